```python
import math
import jax, jax.numpy as jnp
from jax import lax
import numpy as np

D_MODEL = 4096
BATCH = 1
SEQ = 16384
DEPTH = 2
DEC_BATCH = 8
DEC_SEQ = 32
PAST_LEN = 2048

CHUNK = 64
ATTN_WIDTH = D_MODEL // 2
CONV_CH = D_MODEL - ATTN_WIDTH
HEAD_DIM = 128
N_HEADS = ATTN_WIDTH // (2 * HEAD_DIM)
QK_DIM = 2 * HEAD_DIM
V_DIM = 2 * HEAD_DIM
ROT_DIM = HEAD_DIM // 4
ROPE_THETA = 500000.0
CONV_WIDTH = 31
CONV_STATE = CONV_WIDTH - 1
N_GROUPS = 4
EXPERTS_PER_GROUP = 8
N_EXPERTS = N_GROUPS * EXPERTS_PER_GROUP
TOP_K_IN_GROUP = 2
EXPERT_FF = 512
Q_BLOCK = 128
IN_COLS = 3 * ATTN_WIDTH + 2 * CONV_CH
EPS = 1e-6
NEG_INF = -1e30

kernel_name = "hybrid_diffattn_conformer_hiermoe_stream_step"


def rms_norm(x, g):
    xf = x.astype(jnp.float32)
    y = xf * lax.rsqrt(jnp.mean(xf * xf, axis=-1, keepdims=True) + EPS)
    return (y * g.astype(jnp.float32)).astype(x.dtype)


def layer_norm(x, g, b):
    xf = x.astype(jnp.float32)
    mu = jnp.mean(xf, axis=-1, keepdims=True)
    var = jnp.mean(jnp.square(xf - mu), axis=-1, keepdims=True)
    y = (xf - mu) * lax.rsqrt(var + EPS) * g.astype(jnp.float32) + b.astype(jnp.float32)
    return y.astype(x.dtype)


def partial_rope(x, pos):
    half = ROT_DIM // 2
    inv_freq = 1.0 / (ROPE_THETA ** (jnp.arange(half, dtype=jnp.float32) * 2.0 / ROT_DIM))
    ang = pos.astype(jnp.float32)[:, None] * inv_freq[None, :]
    cos = jnp.cos(ang)[None, :, None, None, :]
    sin = jnp.sin(ang)[None, :, None, None, :]
    xf = x.astype(jnp.float32)
    x1, x2, rest = xf[..., :half], xf[..., half:ROT_DIM], xf[..., ROT_DIM:]
    out = jnp.concatenate([x1 * cos - x2 * sin, x2 * cos + x1 * sin, rest], axis=-1)
    return out.astype(x.dtype)


def diff_attend(q, k, v, mask, lam):
    s = jnp.einsum("bqhcd,bkhcd->bhcqk", q, k).astype(jnp.float32) * (HEAD_DIM ** -0.5)
    if mask is not None:
        s = jnp.where(mask, s, NEG_INF)
    p = jax.nn.softmax(s, axis=-1)
    a = p[:, :, 0] - lam * p[:, :, 1]
    return jnp.einsum("bhqk,bkhe->bqhe", a.astype(v.dtype), v)


def chunk_causal_diff_attention(q, k, v, lam):
    B, S = q.shape[0], q.shape[1]
    nb = S // Q_BLOCK
    qb = q.reshape(B, nb, Q_BLOCK, N_HEADS, 2, HEAD_DIM).transpose(1, 0, 2, 3, 4, 5)
    k_chunk = jnp.arange(S) // CHUNK

    def one_block(args):
        q_blk, start = args
        q_chunk = (start + jnp.arange(Q_BLOCK)) // CHUNK
        mask = k_chunk[None, :] <= q_chunk[:, None]
        return diff_attend(q_blk, k, v, mask, lam)

    out = lax.map(one_block, (qb, jnp.arange(nb) * Q_BLOCK))
    return out.transpose(1, 0, 2, 3, 4).reshape(B, S, N_HEADS, V_DIM)


def conformer_conv(glu_a, glu_b, conv_state, conv_w, conv_b, ln_g, ln_b, w_pw2):
    u = glu_a * jax.nn.sigmoid(glu_b)
    buf = jnp.concatenate([conv_state.astype(u.dtype), u], axis=1)
    new_state = buf[:, -CONV_STATE:]
    y = lax.conv_general_dilated(
        buf, conv_w[:, None, :].astype(u.dtype), (1,), "VALID",
        dimension_numbers=("NWC", "WIO", "NWC"), feature_group_count=CONV_CH) + conv_b
    y = layer_norm(y, ln_g, ln_b)
    y = y * jax.nn.sigmoid(y)
    return y @ w_pw2, new_state


def hierarchical_moe(h, w_gr, b_gr, w_er, b_er, w1, w3, w2):
    B, T, D = h.shape
    hf = h.reshape(B * T, D)
    n = hf.shape[0]
    g_logits = (hf @ w_gr + b_gr).astype(jnp.float32)
    g_prob = jax.nn.softmax(g_logits, axis=-1)
    g_idx = jnp.argmax(g_logits, axis=-1)
    g_gate = jnp.max(g_prob, axis=-1, keepdims=True)
    e_logits = (hf @ w_er + b_er).astype(jnp.float32).reshape(n, N_GROUPS, EXPERTS_PER_GROUP)
    e_in = e_logits[jnp.arange(n), g_idx]
    top_v, top_i = lax.top_k(e_in, TOP_K_IN_GROUP)
    w = jax.nn.softmax(top_v, axis=-1) * g_gate
    e_idx = g_idx[:, None] * EXPERTS_PER_GROUP + top_i
    gates = jnp.sum(jax.nn.one_hot(e_idx, N_EXPERTS, dtype=jnp.float32) * w[..., None], axis=1)
    out = jnp.zeros((n, D), jnp.float32)
    for e in range(N_EXPERTS):
        he = jax.nn.silu(hf @ w1[e]) * (hf @ w3[e])
        out = out + gates[:, e:e + 1] * (he @ w2[e]).astype(jnp.float32)
    return out.astype(h.dtype).reshape(B, T, D)


def trunk_layer(x, pos, past_k, past_v, conv_state, lam, lam_init, p):
    B, T = x.shape[0], x.shape[1]
    A, C = ATTN_WIDTH, CONV_CH
    h = rms_norm(x, p["norm1_g"])
    proj = h @ p["w_in"]
    q = proj[..., :A].reshape(B, T, N_HEADS, 2, HEAD_DIM)
    k = proj[..., A:2 * A].reshape(B, T, N_HEADS, 2, HEAD_DIM)
    v = proj[..., 2 * A:3 * A].reshape(B, T, N_HEADS, V_DIM)
    q = partial_rope(rms_norm(q, p["q_norm_g"]), pos)
    k = partial_rope(rms_norm(k, p["k_norm_g"]), pos)
    if past_k is None:
        o = chunk_causal_diff_attention(q, k, v, lam)
    else:
        k_all = jnp.concatenate([past_k.reshape(B, -1, N_HEADS, 2, HEAD_DIM).astype(k.dtype), k], axis=1)
        v_all = jnp.concatenate([past_v.astype(v.dtype), v], axis=1)
        o = diff_attend(q, k_all, v_all, None, lam)
    o = (rms_norm(o, p["subln_g"]) * (1.0 - lam_init)).reshape(B, T, A)
    c, new_conv = conformer_conv(proj[..., 3 * A:3 * A + C], proj[..., 3 * A + C:], conv_state,
                                 p["conv_w"], p["conv_b"], p["conv_ln_g"], p["conv_ln_b"], p["w_pw2"])
    x = x + jnp.concatenate([o, c], axis=-1) @ p["w_out"]
    x = x + hierarchical_moe(rms_norm(x, p["norm2_g"]), p["w_group_router"], p["b_group_router"],
                             p["w_expert_router"], p["b_expert_router"], p["w1"], p["w3"], p["w2"])
    return x, k.reshape(B, T, N_HEADS, QK_DIM), v, new_conv


def setup_inputs(seed: int = 0) -> dict:
    key = jax.random.key(seed)
    ks = jax.random.split(key, 32)
    f32 = jnp.float32

    def nrm(k, shape, scale):
        return jax.random.normal(k, shape, f32) * scale

    return {
        "x_prompt": nrm(ks[0], (BATCH, SEQ, D_MODEL), 1.0),
        "x_sample": nrm(ks[1], (DEC_BATCH, DEC_SEQ, D_MODEL), 1.0),
        "cache_k": nrm(ks[2], (DEPTH, DEC_BATCH, PAST_LEN, N_HEADS, QK_DIM), 1.0),
        "cache_v": nrm(ks[3], (DEPTH, DEC_BATCH, PAST_LEN, N_HEADS, V_DIM), 1.0),
        "state_conv": nrm(ks[4], (DEPTH, DEC_BATCH, CONV_STATE, CONV_CH), 0.5),
        "norm1_g": 1.0 + nrm(ks[5], (DEPTH, D_MODEL), 0.02),
        "w_in": nrm(ks[6], (DEPTH, D_MODEL, IN_COLS), D_MODEL ** -0.5),
        "q_norm_g": 1.0 + nrm(ks[7], (DEPTH, HEAD_DIM), 0.02),
        "k_norm_g": 1.0 + nrm(ks[8], (DEPTH, HEAD_DIM), 0.02),
        "lambda_q1": nrm(ks[9], (DEPTH, HEAD_DIM), 0.1),
        "lambda_k1": nrm(ks[10], (DEPTH, HEAD_DIM), 0.1),
        "lambda_q2": nrm(ks[11], (DEPTH, HEAD_DIM), 0.1),
        "lambda_k2": nrm(ks[12], (DEPTH, HEAD_DIM), 0.1),
        "subln_g": 1.0 + nrm(ks[13], (DEPTH, V_DIM), 0.02),
        "conv_w": nrm(ks[14], (DEPTH, CONV_WIDTH, CONV_CH), CONV_WIDTH ** -0.5),
        "conv_b": nrm(ks[15], (DEPTH, CONV_CH), 0.01),
        "conv_ln_g": 1.0 + nrm(ks[16], (DEPTH, CONV_CH), 0.02),
        "conv_ln_b": nrm(ks[17], (DEPTH, CONV_CH), 0.01),
        "w_pw2": nrm(ks[18], (DEPTH, CONV_CH, CONV_CH), CONV_CH ** -0.5),
        "w_out": nrm(ks[19], (DEPTH, D_MODEL, D_MODEL), D_MODEL ** -0.5),
        "norm2_g": 1.0 + nrm(ks[20], (DEPTH, D_MODEL), 0.02),
        "w_group_router": nrm(ks[21], (DEPTH, D_MODEL, N_GROUPS), D_MODEL ** -0.5),
        "b_group_router": nrm(ks[22], (DEPTH, N_GROUPS), 0.01),
        "w_expert_router": nrm(ks[23], (DEPTH, D_MODEL, N_EXPERTS), D_MODEL ** -0.5),
        "b_expert_router": nrm(ks[24], (DEPTH, N_EXPERTS), 0.01),
        "w1": nrm(ks[25], (DEPTH, N_EXPERTS, D_MODEL, EXPERT_FF), D_MODEL ** -0.5),
        "w3": nrm(ks[26], (DEPTH, N_EXPERTS, D_MODEL, EXPERT_FF), D_MODEL ** -0.5),
        "w2": nrm(ks[27], (DEPTH, N_EXPERTS, EXPERT_FF, D_MODEL), EXPERT_FF ** -0.5),
    }


def reference(x_prompt, x_sample, cache_k, cache_v, state_conv, norm1_g, w_in, q_norm_g, k_norm_g,
              lambda_q1, lambda_k1, lambda_q2, lambda_k2, subln_g, conv_w, conv_b, conv_ln_g, conv_ln_b,
              w_pw2, w_out, norm2_g, w_group_router, b_group_router, w_expert_router, b_expert_router,
              w1, w3, w2):
    f32 = jnp.float32
    pos_p = jnp.arange(x_prompt.shape[1])
    pos_s = cache_k.shape[2] + jnp.arange(x_sample.shape[1])
    conv_zero = jnp.zeros((x_prompt.shape[0], CONV_STATE, CONV_CH), x_prompt.dtype)
    yp, ys = x_prompt, x_sample
    kp_l, vp_l, cp_l, ks_l, vs_l, cs_l = [], [], [], [], [], []
    for l in range(DEPTH):
        lam_init = 0.8 - 0.6 * math.exp(-0.3 * l)
        lam = (jnp.exp(jnp.sum(lambda_q1[l].astype(f32) * lambda_k1[l].astype(f32)))
               - jnp.exp(jnp.sum(lambda_q2[l].astype(f32) * lambda_k2[l].astype(f32))) + lam_init)
        p = {
            "norm1_g": norm1_g[l], "w_in": w_in[l], "q_norm_g": q_norm_g[l], "k_norm_g": k_norm_g[l],
            "subln_g": subln_g[l], "conv_w": conv_w[l], "conv_b": conv_b[l], "conv_ln_g": conv_ln_g[l],
            "conv_ln_b": conv_ln_b[l], "w_pw2": w_pw2[l], "w_out": w_out[l], "norm2_g": norm2_g[l],
            "w_group_router": w_group_router[l], "b_group_router": b_group_router[l],
            "w_expert_router": w_expert_router[l], "b_expert_router": b_expert_router[l],
            "w1": w1[l], "w3": w3[l], "w2": w2[l],
        }
        yp, kp, vp, cp = trunk_layer(yp, pos_p, None, None, conv_zero, lam, lam_init, p)
        ys, kn, vn, cn = trunk_layer(ys, pos_s, cache_k[l], cache_v[l], state_conv[l], lam, lam_init, p)
        kp_l.append(kp); vp_l.append(vp); cp_l.append(cp)
        ks_l.append(kn); vs_l.append(vn); cs_l.append(cn)
    return (yp, ys, jnp.stack(kp_l), jnp.stack(vp_l), jnp.stack(cp_l),
            jnp.stack(ks_l), jnp.stack(vs_l), jnp.stack(cs_l))
```

```python
import functools
import math

import jax
import jax.numpy as jnp
from jax import lax
from jax.experimental import pallas as pl
from jax.experimental.pallas import tpu as pltpu

F32 = jnp.float32
BF16 = jnp.bfloat16

HEAD_DIM = 128
ROT_DIM = HEAD_DIM // 4
ROPE_THETA = 500000.0
CHUNK = 64
CONV_WIDTH = 31
CONV_STATE = CONV_WIDTH - 1
TOP_K_IN_GROUP = 2
EPS = 1e-6
NEG_INF = -1e30

LANES = 128
SUBLANES = 8
VMEM_LIMIT_BYTES = 56 * 1024 * 1024

CONV_HALO = 32
ROUTE_LANES = LANES


def _params(*semantics):
    return pltpu.CompilerParams(dimension_semantics=semantics, vmem_limit_bytes=VMEM_LIMIT_BYTES)


def _row_tile(t, cap):
    tile = min(t, cap)
    assert t % tile == 0, (t, tile)
    return tile


def _rmsnorm_kernel(x_ref, g_ref, o_ref):
    x = x_ref[...]
    ms = jnp.mean(x * x, axis=-1, keepdims=True)
    o_ref[...] = (x * lax.rsqrt(ms + EPS) * g_ref[...]).astype(o_ref.dtype)


def _rmsnorm(x, g):
    t, d = x.shape
    tm = _row_tile(t, 256)
    return pl.pallas_call(
        _rmsnorm_kernel,
        grid=(t // tm,),
        in_specs=[pl.BlockSpec((tm, d), lambda i: (i, 0)), pl.BlockSpec((1, d), lambda i: (0, 0))],
        out_specs=pl.BlockSpec((tm, d), lambda i: (i, 0)),
        out_shape=jax.ShapeDtypeStruct((t, d), BF16),
        compiler_params=_params("parallel"),
        name="rmsnorm",
    )(x, g.reshape(1, d))


def _rope_tables(pos):
    half = ROT_DIM // 2
    inv_freq = 1.0 / (ROPE_THETA ** (jnp.arange(half, dtype=F32) * 2.0 / ROT_DIM))
    ang = pos.astype(F32)[:, None] * inv_freq[None, :]
    cos, sin = jnp.cos(ang), jnp.sin(ang)
    t = pos.shape[0]
    zeros = jnp.zeros((t, half), F32)
    rest0 = jnp.zeros((t, HEAD_DIM - ROT_DIM), F32)
    c = jnp.concatenate([cos, cos, jnp.ones((t, HEAD_DIM - ROT_DIM), F32)], axis=-1)
    sa = jnp.concatenate([-sin, zeros, rest0], axis=-1)
    sb = jnp.concatenate([zeros, sin, rest0], axis=-1)
    return c, sa, sb


def _qk_proj_kernel(h_ref, w_ref, g_ref, c_ref, sa_ref, sb_ref, *out_refs, scale):
    acc = jnp.dot(h_ref[...], w_ref[...], preferred_element_type=F32)
    g, c, sa, sb = g_ref[...], c_ref[...], sa_ref[...], sb_ref[...]
    for j in range(acc.shape[1] // HEAD_DIM):
        cols = slice(j * HEAD_DIM, (j + 1) * HEAD_DIM)
        x = acc[:, cols]
        ms = jnp.mean(x * x, axis=-1, keepdims=True)
        y = x * lax.rsqrt(ms + EPS) * g
        r = y * c + pltpu.roll(y, HEAD_DIM - ROT_DIM // 2, 1) * sa + pltpu.roll(y, ROT_DIM // 2, 1) * sb
        if scale != 1.0:
            r = r * scale
        for o_ref in out_refs:
            o_ref[:, cols] = r.astype(o_ref.dtype)


def _qk_proj(h, w_bf, layer, col0, width, gain, tables, scale, out_dtypes):
    t, d = h.shape
    tm = _row_tile(t, 1024)
    tn = min(width, 512)
    assert width % tn == 0 and col0 % tn == 0
    jb = col0 // tn
    c, sa, sb = tables
    tab_spec = pl.BlockSpec((tm, HEAD_DIM), lambda i, j: (i, 0))
    return pl.pallas_call(
        functools.partial(_qk_proj_kernel, scale=scale),
        grid=(t // tm, width // tn),
        in_specs=[
            pl.BlockSpec((tm, d), lambda i, j: (i, 0)),
            pl.BlockSpec((None, d, tn), lambda i, j: (layer, 0, jb + j)),
            pl.BlockSpec((1, HEAD_DIM), lambda i, j: (0, 0)),
            tab_spec, tab_spec, tab_spec,
        ],
        out_specs=[pl.BlockSpec((tm, tn), lambda i, j: (i, j)) for _ in out_dtypes],
        out_shape=[jax.ShapeDtypeStruct((t, width), dt) for dt in out_dtypes],
        compiler_params=_params("parallel", "parallel"),
        name="qk_proj",
    )(h, w_bf, gain.reshape(1, HEAD_DIM), c, sa, sb)


def _v_proj_kernel(h_ref, w_ref, o32_ref, o16_ref):
    acc = jnp.dot(h_ref[...], w_ref[...], preferred_element_type=F32)
    o32_ref[...] = acc
    o16_ref[...] = acc.astype(o16_ref.dtype)


def _v_proj(h, w_bf, layer, col0, width):
    t, d = h.shape
    tm = _row_tile(t, 1024)
    tn = min(width, 512)
    assert width % tn == 0 and col0 % tn == 0
    jb = col0 // tn
    return pl.pallas_call(
        _v_proj_kernel,
        grid=(t // tm, width // tn),
        in_specs=[
            pl.BlockSpec((tm, d), lambda i, j: (i, 0)),
            pl.BlockSpec((None, d, tn), lambda i, j: (layer, 0, jb + j)),
        ],
        out_specs=[pl.BlockSpec((tm, tn), lambda i, j: (i, j))] * 2,
        out_shape=[jax.ShapeDtypeStruct((t, width), F32), jax.ShapeDtypeStruct((t, width), BF16)],
        compiler_params=_params("parallel", "parallel"),
        name="v_proj",
    )(h, w_bf)


def _glu_proj_kernel(h_ref, wa_ref, wb_ref, u_ref):
    h = h_ref[...]
    a = jnp.dot(h, wa_ref[...], preferred_element_type=F32)
    b = jnp.dot(h, wb_ref[...], preferred_element_type=F32)
    u_ref[...] = a * jax.nn.sigmoid(b)


def _glu_proj(h, w_bf, layer, col_a, col_b, width):
    t, d = h.shape
    tm = _row_tile(t, 1024)
    tn = min(width, 512)
    assert width % tn == 0 and col_a % tn == 0 and col_b % tn == 0
    ja, jb = col_a // tn, col_b // tn
    return pl.pallas_call(
        _glu_proj_kernel,
        grid=(t // tm, width // tn),
        in_specs=[
            pl.BlockSpec((tm, d), lambda i, j: (i, 0)),
            pl.BlockSpec((None, d, tn), lambda i, j: (layer, 0, ja + j)),
            pl.BlockSpec((None, d, tn), lambda i, j: (layer, 0, jb + j)),
        ],
        out_specs=pl.BlockSpec((tm, tn), lambda i, j: (i, j)),
        out_shape=jax.ShapeDtypeStruct((t, width), F32),
        compiler_params=_params("parallel", "parallel"),
        name="glu_proj",
    )(h, w_bf, w_bf)


def _stack_components(qs_ref, q, tq):
    z = jnp.zeros((tq, HEAD_DIM), q.dtype)
    qs_ref[0:tq, 0:HEAD_DIM] = q[:, 0:HEAD_DIM]
    qs_ref[0:tq, HEAD_DIM:] = z
    qs_ref[tq:, 0:HEAD_DIM] = z
    qs_ref[tq:, HEAD_DIM:] = q[:, HEAD_DIM:]


def _scores(qs, k):
    return lax.dot_general(qs, k, (((1,), (1,)), ((), ())), preferred_element_type=F32)


def _diff_subnorm(o1, o2, lam, g, out_scale):
    o = o1 - lam * o2
    ms = jnp.mean(o * o, axis=-1, keepdims=True)
    return o * lax.rsqrt(ms + EPS) * g * out_scale


def _attn_prompt_kernel(lam_ref, q_ref, k_ref, v_ref, g_ref, o_ref, qs_ref, m_ref, l_ref, acc_ref, *, tq,
                        out_scale):
    qi = pl.program_id(1)
    _stack_components(qs_ref, q_ref[...], tq)
    m_ref[...] = jnp.full(m_ref.shape, NEG_INF, F32)
    l_ref[...] = jnp.zeros(l_ref.shape, F32)
    acc_ref[...] = jnp.zeros(acc_ref.shape, F32)

    def step(ki, masked):
        start = pl.multiple_of(ki * tq, tq)
        k = k_ref[pl.ds(start, tq), :]
        v = v_ref[pl.ds(start, tq), :]
        s = _scores(qs_ref[...], k)
        if masked:
            row = lax.broadcasted_iota(jnp.int32, s.shape, 0)
            col = lax.broadcasted_iota(jnp.int32, s.shape, 1)
            row = jnp.where(row >= tq, row - tq, row)
            s = jnp.where((col // CHUNK) <= (row // CHUNK), s, NEG_INF)
        m_prev = m_ref[...]
        m_new = jnp.maximum(m_prev, jnp.max(s, axis=1, keepdims=True))
        alpha = jnp.exp(m_prev - m_new)
        p = jnp.exp(s - m_new)
        l_ref[...] = alpha * l_ref[...] + jnp.sum(p, axis=1, keepdims=True)
        acc_ref[...] = alpha * acc_ref[...] + jnp.dot(p.astype(v.dtype), v, preferred_element_type=F32)
        m_ref[...] = m_new

    def body(ki, carry):
        step(ki, False)
        return carry

    lax.fori_loop(0, qi, body, 0)
    step(qi, True)

    o = acc_ref[...] / l_ref[...]
    o_ref[...] = _diff_subnorm(o[0:tq], o[tq:], lam_ref[0], g_ref[...], out_scale).astype(o_ref.dtype)


def _attn_prompt(q, k, v, lam, subln_g, out_scale):
    t, a = q.shape
    qk = 2 * HEAD_DIM
    nh = a // qk
    tq = _row_tile(t, 512)
    assert tq % CHUNK == 0
    return pl.pallas_call(
        functools.partial(_attn_prompt_kernel, tq=tq, out_scale=out_scale),
        grid=(nh, t // tq),
        in_specs=[
            pl.BlockSpec(memory_space=pltpu.SMEM),
            pl.BlockSpec((tq, qk), lambda h, i: (i, h)),
            pl.BlockSpec((t, qk), lambda h, i: (0, h)),
            pl.BlockSpec((t, qk), lambda h, i: (0, h)),
            pl.BlockSpec((1, qk), lambda h, i: (0, 0)),
        ],
        out_specs=pl.BlockSpec((tq, qk), lambda h, i: (i, h)),
        out_shape=jax.ShapeDtypeStruct((t, a), BF16),
        scratch_shapes=[
            pltpu.VMEM((2 * tq, qk), BF16),
            pltpu.VMEM((2 * tq, 1), F32),
            pltpu.VMEM((2 * tq, 1), F32),
            pltpu.VMEM((2 * tq, qk), F32),
        ],
        compiler_params=_params("parallel", "arbitrary"),
        name="attn_prompt",
    )(lam, q, k, v, subln_g.reshape(1, qk))


def _attn_sample_kernel(lam_ref, q_ref, kn_ref, vn_ref, kc_ref, vc_ref, g_ref, o_ref, qs_ref, *, ts, out_scale):
    _stack_components(qs_ref, q_ref[...], ts)
    qs = qs_ref[...]
    kc = kc_ref[...].astype(BF16)
    vc = vc_ref[...].astype(BF16)
    s_c = _scores(qs, kc)
    s_n = _scores(qs, kn_ref[...])
    m = jnp.maximum(jnp.max(s_c, axis=1, keepdims=True), jnp.max(s_n, axis=1, keepdims=True))
    p_c = jnp.exp(s_c - m)
    p_n = jnp.exp(s_n - m)
    l = jnp.sum(p_c, axis=1, keepdims=True) + jnp.sum(p_n, axis=1, keepdims=True)
    acc = jnp.dot(p_c.astype(BF16), vc, preferred_element_type=F32)
    acc = acc + jnp.dot(p_n.astype(BF16), vn_ref[...], preferred_element_type=F32)
    o = acc / l
    o_ref[...] = _diff_subnorm(o[0:ts], o[ts:], lam_ref[0], g_ref[...], out_scale).astype(o_ref.dtype)


def _attn_sample(q, k, v, cache_k, cache_v, layer, lam, subln_g, out_scale):
    _, nb, past, a = cache_k.shape
    ts = q.shape[0] // nb
    qk = 2 * HEAD_DIM
    nh = a // qk
    new_spec = pl.BlockSpec((ts, qk), lambda b, h: (b, h))
    cache_spec = pl.BlockSpec((None, None, past, qk), lambda b, h: (layer, b, 0, h))
    return pl.pallas_call(
        functools.partial(_attn_sample_kernel, ts=ts, out_scale=out_scale),
        grid=(nb, nh),
        in_specs=[pl.BlockSpec(memory_space=pltpu.SMEM), new_spec, new_spec, new_spec, cache_spec, cache_spec,
                  pl.BlockSpec((1, qk), lambda b, h: (0, 0))],
        out_specs=new_spec,
        out_shape=jax.ShapeDtypeStruct(q.shape, BF16),
        scratch_shapes=[pltpu.VMEM((2 * ts, qk), BF16)],
        compiler_params=_params("parallel", "parallel"),
        name="attn_sample",
    )(lam, q, k, v, cache_k, cache_v, subln_g.reshape(1, qk))


CONV_LANE_CHUNK = 128


def _conv_kernel(u_ref, prev_ref, state_ref, cw_ref, cb_ref, g_ref, b_ref, w_ref, o_ref, buf_ref, y_ref, *, tm,
                 tiles_per_seq):
    first = (pl.program_id(0) % tiles_per_seq) == 0
    lo = CONV_HALO - CONV_STATE

    @pl.when(first)
    def _():
        buf_ref[lo:CONV_HALO, :] = state_ref[...]

    @pl.when(jnp.logical_not(first))
    def _():
        buf_ref[lo:CONV_HALO, :] = prev_ref[lo:CONV_HALO, :]

    buf_ref[CONV_HALO:, :] = u_ref[...]
    nch = u_ref.shape[1]

    def chunk(ci, carry):
        c0 = pl.multiple_of(ci * CONV_LANE_CHUNK, CONV_LANE_CHUNK)
        lanes = pl.ds(c0, CONV_LANE_CHUNK)
        acc = jnp.broadcast_to(cb_ref[:, lanes], (tm, CONV_LANE_CHUNK))
        for w in range(CONV_WIDTH):
            acc = acc + buf_ref[lo + w:lo + w + tm, lanes] * cw_ref[w:w + 1, lanes]
        y_ref[:, lanes] = acc
        return carry

    lax.fori_loop(0, nch // CONV_LANE_CHUNK, chunk, 0)

    y = y_ref[...]
    mu = jnp.mean(y, axis=-1, keepdims=True)
    yc = y - mu
    var = jnp.mean(yc * yc, axis=-1, keepdims=True)
    yn = yc * lax.rsqrt(var + EPS) * g_ref[...] + b_ref[...]
    sw = yn * jax.nn.sigmoid(yn)
    o_ref[...] = jnp.dot(sw.astype(BF16), w_ref[...], preferred_element_type=F32).astype(o_ref.dtype)


def _conv_module(u, state, nseq, conv_w, conv_b, ln_g, ln_b, w_pw2_bf, layer):
    t, ch = u.shape
    seq_len = t // nseq
    tm = _row_tile(seq_len, 256)
    assert tm % CONV_HALO == 0 and ch % CONV_LANE_CHUNK == 0
    tiles_per_seq = seq_len // tm
    halo_blocks = tm // CONV_HALO
    vec = lambda x: x.reshape(1, ch)
    const = lambda i: (0, 0)
    return pl.pallas_call(
        functools.partial(_conv_kernel, tm=tm, tiles_per_seq=tiles_per_seq),
        grid=(t // tm,),
        in_specs=[
            pl.BlockSpec((tm, ch), lambda i: (i, 0)),
            pl.BlockSpec((CONV_HALO, ch), lambda i: (jnp.maximum(i * halo_blocks - 1, 0), 0)),
            pl.BlockSpec((None, CONV_STATE, ch), lambda i: (i // tiles_per_seq, 0, 0)),
            pl.BlockSpec((CONV_WIDTH, ch), const),
            pl.BlockSpec((1, ch), const), pl.BlockSpec((1, ch), const), pl.BlockSpec((1, ch), const),
            pl.BlockSpec((None, ch, ch), lambda i: (layer, 0, 0)),
        ],
        out_specs=pl.BlockSpec((tm, ch), lambda i: (i, 0)),
        out_shape=jax.ShapeDtypeStruct((t, ch), BF16),
        scratch_shapes=[pltpu.VMEM((CONV_HALO + tm, ch), F32), pltpu.VMEM((tm, ch), F32)],
        compiler_params=_params("parallel"),
        name="conv_module",
    )(u, u, state, conv_w, vec(conv_b), vec(ln_g), vec(ln_b), w_pw2_bf)


def _out_proj_kernel(x_ref, o_ref, c_ref, wt_ref, wb_ref, y_ref):
    acc = jnp.dot(o_ref[...], wt_ref[...], preferred_element_type=F32)
    acc = acc + jnp.dot(c_ref[...], wb_ref[...], preferred_element_type=F32)
    y_ref[...] = x_ref[...] + acc


def _out_proj(x, o, c, w_out_bf, layer):
    t, d = x.shape
    a = o.shape[1]
    assert c.shape[1] == a and 2 * a == d
    tm = _row_tile(t, 1024)
    tn = min(d, 512)
    return pl.pallas_call(
        _out_proj_kernel,
        grid=(t // tm, d // tn),
        in_specs=[
            pl.BlockSpec((tm, tn), lambda i, j: (i, j)),
            pl.BlockSpec((tm, a), lambda i, j: (i, 0)),
            pl.BlockSpec((tm, a), lambda i, j: (i, 0)),
            pl.BlockSpec((None, a, tn), lambda i, j: (layer, 0, j)),
            pl.BlockSpec((None, a, tn), lambda i, j: (layer, 1, j)),
        ],
        out_specs=pl.BlockSpec((tm, tn), lambda i, j: (i, j)),
        out_shape=jax.ShapeDtypeStruct((t, d), F32),
        compiler_params=_params("parallel", "parallel"),
        name="out_proj",
    )(x, o, c, w_out_bf, w_out_bf)


def _router_kernel(x_ref, g_ref, wr_ref, br_ref, h_ref, ri_ref, rw_ref, cnt_ref, carry_ref, *, n_experts, n_groups,
                   per_group):
    step = pl.program_id(0)

    @pl.when(step == 0)
    def _():
        carry_ref[...] = jnp.zeros(carry_ref.shape, F32)

    x = x_ref[...]
    ms = jnp.mean(x * x, axis=-1, keepdims=True)
    h = x * lax.rsqrt(ms + EPS) * g_ref[...]
    h_ref[...] = h
    logits = jnp.dot(h, wr_ref[...], preferred_element_type=F32, precision=lax.Precision.HIGHEST) + br_ref[...]

    tm = x.shape[0]
    lane = lax.broadcasted_iota(jnp.int32, (tm, ROUTE_LANES), 1)
    big = jnp.int32(ROUTE_LANES)
    is_group = (lane >= n_experts) & (lane < n_experts + n_groups)
    gl = jnp.where(is_group, logits, -jnp.inf)
    gmax = jnp.max(gl, axis=1, keepdims=True)
    gidx = jnp.min(jnp.where(gl == gmax, lane, big), axis=1, keepdims=True) - n_experts
    g_gate = 1.0 / jnp.sum(jnp.exp(gl - gmax), axis=1, keepdims=True)

    in_group = (lane < n_experts) & ((lane // per_group) == gidx)
    e_in = jnp.where(in_group, logits, -jnp.inf)
    m1 = jnp.max(e_in, axis=1, keepdims=True)
    i1 = jnp.min(jnp.where(e_in == m1, lane, big), axis=1, keepdims=True)
    e_in2 = jnp.where(lane == i1, -jnp.inf, e_in)
    m2 = jnp.max(e_in2, axis=1, keepdims=True)
    i2 = jnp.min(jnp.where(e_in2 == m2, lane, big), axis=1, keepdims=True)
    t = jnp.exp(m2 - m1)
    w1 = g_gate / (1.0 + t)
    w2 = g_gate * t / (1.0 + t)

    sel1 = lane == i1
    sel2 = lane == i2
    onehot = jnp.where(sel1 | sel2, 1.0, 0.0).astype(BF16)
    r = lax.broadcasted_iota(jnp.int32, (tm, tm), 0)
    c = lax.broadcasted_iota(jnp.int32, (tm, tm), 1)
    lower = jnp.where(c < r, 1.0, 0.0).astype(BF16)
    before = jnp.dot(lower, onehot, preferred_element_type=F32) + carry_ref[...]
    rank1 = jnp.sum(jnp.where(sel1, before, 0.0), axis=1, keepdims=True).astype(jnp.int32)
    rank2 = jnp.sum(jnp.where(sel2, before, 0.0), axis=1, keepdims=True).astype(jnp.int32)
    carry_ref[...] = carry_ref[...] + jnp.sum(onehot.astype(F32), axis=0, keepdims=True)
    cnt_ref[...] = carry_ref[...].astype(jnp.int32)

    ri = jnp.where(lane == 0, i1, jnp.where(lane == 1, i2, jnp.where(lane == 2, rank1,
                                                                     jnp.where(lane == 3, rank2, 0))))
    ri_ref[...] = ri
    rw_ref[...] = jnp.where(lane == 0, w1, jnp.where(lane == 1, w2, 0.0))


def _router(x, norm_g, w_route, b_route, n_experts, n_groups):
    t, d = x.shape
    tm = _row_tile(t, 256)
    const = lambda i: (0, 0)
    return pl.pallas_call(
        functools.partial(_router_kernel, n_experts=n_experts, n_groups=n_groups,
                          per_group=n_experts // n_groups),
        grid=(t // tm,),
        in_specs=[
            pl.BlockSpec((tm, d), lambda i: (i, 0)),
            pl.BlockSpec((1, d), const),
            pl.BlockSpec((d, ROUTE_LANES), const),
            pl.BlockSpec((1, ROUTE_LANES), const),
        ],
        out_specs=[
            pl.BlockSpec((tm, d), lambda i: (i, 0)),
            pl.BlockSpec((tm, ROUTE_LANES), lambda i: (i, 0)),
            pl.BlockSpec((tm, ROUTE_LANES), lambda i: (i, 0)),
            pl.BlockSpec((1, ROUTE_LANES), const),
        ],
        out_shape=[
            jax.ShapeDtypeStruct((t, d), F32),
            jax.ShapeDtypeStruct((t, ROUTE_LANES), jnp.int32),
            jax.ShapeDtypeStruct((t, ROUTE_LANES), F32),
            jax.ShapeDtypeStruct((1, ROUTE_LANES), jnp.int32),
        ],
        scratch_shapes=[pltpu.VMEM((1, ROUTE_LANES), F32)],
        compiler_params=_params("arbitrary"),
        name="router",
    )(x, norm_g.reshape(1, d), w_route, b_route)


def _row_copy(src_ref, src_row, dst_ref, dst_row, sem):
    return pltpu.make_async_copy(src_ref.at[pl.ds(src_row, 1)], dst_ref.at[pl.ds(dst_row, 1)], sem)


def _dispatch_kernel(pos_ref, h_ref, xs_in_ref, xs_ref, sem, *, tb):
    del xs_in_ref
    n = TOP_K_IN_GROUP * tb

    def start(r, carry):
        _row_copy(h_ref, r // TOP_K_IN_GROUP, xs_ref, pos_ref[0, 0, r], sem).start()
        return carry

    def wait(r, carry):
        _row_copy(h_ref, r // TOP_K_IN_GROUP, xs_ref, pos_ref[0, 0, r], sem).wait()
        return carry

    lax.fori_loop(0, n, start, 0)
    lax.fori_loop(0, n, wait, 0)


def _dispatch(h, pos, n_slots):
    t, d = h.shape
    tb = _row_tile(t, 256)
    nb = t // tb
    xs0 = jnp.zeros((n_slots, d), h.dtype)
    return pl.pallas_call(
        functools.partial(_dispatch_kernel, tb=tb),
        grid=(nb,),
        in_specs=[
            pl.BlockSpec((1, 1, TOP_K_IN_GROUP * tb), lambda i: (i, 0, 0), memory_space=pltpu.SMEM),
            pl.BlockSpec((tb, d), lambda i: (i, 0)),
            pl.BlockSpec(memory_space=pl.ANY),
        ],
        out_specs=pl.BlockSpec(memory_space=pl.ANY),
        out_shape=jax.ShapeDtypeStruct((n_slots, d), h.dtype),
        scratch_shapes=[pltpu.SemaphoreType.DMA(())],
        input_output_aliases={2: 0},
        compiler_params=_params("arbitrary"),
        name="moe_dispatch",
    )(pos.reshape(nb, 1, TOP_K_IN_GROUP * tb), h, xs0)


def _experts_kernel(te_ref, tb_ref, na_ref, x_ref, w1_ref, w3_ref, w2_ref, y_ref):
    del te_ref, tb_ref
    active = pl.program_id(0) < na_ref[0]

    @pl.when(active)
    def _():
        x = x_ref[...].astype(BF16)
        a = jnp.dot(x, w1_ref[...], preferred_element_type=F32)
        b = jnp.dot(x, w3_ref[...], preferred_element_type=F32)
        he = (a * jax.nn.sigmoid(a)) * b
        y_ref[...] = jnp.dot(he.astype(BF16), w2_ref[...], preferred_element_type=F32)

    @pl.when(jnp.logical_not(active))
    def _():
        y_ref[...] = jnp.zeros(y_ref.shape, F32)


def _experts(xs, tile_expert, tile_block, n_active, w1_bf, w3_bf, w2_bf, layer, tm):
    n_slots, d = xs.shape
    ff = w1_bf.shape[-1]
    n_tiles = n_slots // tm
    row = lambda i, te, tb, na: (tb[i], 0)
    grid_spec = pltpu.PrefetchScalarGridSpec(
        num_scalar_prefetch=3,
        grid=(n_tiles,),
        in_specs=[
            pl.BlockSpec((tm, d), row),
            pl.BlockSpec((None, None, d, ff), lambda i, te, tb, na: (layer, te[i], 0, 0)),
            pl.BlockSpec((None, None, d, ff), lambda i, te, tb, na: (layer, te[i], 0, 0)),
            pl.BlockSpec((None, None, ff, d), lambda i, te, tb, na: (layer, te[i], 0, 0)),
        ],
        out_specs=pl.BlockSpec((tm, d), lambda i, te, tb, na: (i, 0)),
    )
    return pl.pallas_call(
        _experts_kernel,
        grid_spec=grid_spec,
        out_shape=jax.ShapeDtypeStruct((n_slots, d), F32),
        compiler_params=_params("arbitrary"),
        name="moe_experts",
    )(tile_expert, tile_block, n_active, xs, w1_bf, w3_bf, w2_bf)


def _combine_kernel(pos_ref, x_ref, rw_ref, ys_ref, o_ref, buf_ref, sem, *, tb):
    n = TOP_K_IN_GROUP * tb

    def copy(r):
        k = r % TOP_K_IN_GROUP
        return _row_copy(ys_ref, pos_ref[0, 0, r], buf_ref.at[k], r // TOP_K_IN_GROUP, sem)

    def start(r, carry):
        copy(r).start()
        return carry

    def wait(r, carry):
        copy(r).wait()
        return carry

    lax.fori_loop(0, n, start, 0)
    lax.fori_loop(0, n, wait, 0)
    rw = rw_ref[...]
    o_ref[...] = x_ref[...] + (rw[:, 0:1] * buf_ref[0] + rw[:, 1:2] * buf_ref[1])


def _combine(x, rw, pos, ys):
    t, d = x.shape
    tb = _row_tile(t, 256)
    nb = t // tb
    return pl.pallas_call(
        functools.partial(_combine_kernel, tb=tb),
        grid=(nb,),
        in_specs=[
            pl.BlockSpec((1, 1, TOP_K_IN_GROUP * tb), lambda i: (i, 0, 0), memory_space=pltpu.SMEM),
            pl.BlockSpec((tb, d), lambda i: (i, 0)),
            pl.BlockSpec((tb, ROUTE_LANES), lambda i: (i, 0)),
            pl.BlockSpec(memory_space=pl.ANY),
        ],
        out_specs=pl.BlockSpec((tb, d), lambda i: (i, 0)),
        out_shape=jax.ShapeDtypeStruct((t, d), F32),
        scratch_shapes=[pltpu.VMEM((TOP_K_IN_GROUP, tb, d), F32), pltpu.SemaphoreType.DMA(())],
        compiler_params=_params("arbitrary"),
        name="moe_combine",
    )(pos.reshape(nb, 1, TOP_K_IN_GROUP * tb), x, rw, ys)


def _moe_block(x, norm_g, w_route, b_route, w1_bf, w3_bf, w2_bf, layer, n_experts, n_groups):
    t, d = x.shape
    h, ri, rw, counts = _router(x, norm_g, w_route, b_route, n_experts, n_groups)

    tm = _row_tile(t, 256)
    counts = counts[0, :n_experts]
    padded = ((counts + tm - 1) // tm) * tm
    ends = jnp.cumsum(padded)
    offsets = ends - padded
    n_tiles = (TOP_K_IN_GROUP * t) // tm + n_experts
    n_active = (ends[-1] // tm).astype(jnp.int32)
    tile_ids = jnp.arange(n_tiles, dtype=jnp.int32)
    tile_block = jnp.minimum(tile_ids, n_active - 1)
    tile_expert = jnp.minimum(jnp.searchsorted(ends, tile_block * tm, side="right"), n_experts - 1).astype(jnp.int32)
    pos = jnp.take(offsets, ri[:, 0:TOP_K_IN_GROUP]) + ri[:, TOP_K_IN_GROUP:2 * TOP_K_IN_GROUP]
    pos = pos.astype(jnp.int32)

    xs = _dispatch(h, pos, n_tiles * tm)
    ys = _experts(xs, tile_expert, tile_block, n_active.reshape(1), w1_bf, w3_bf, w2_bf, layer, tm)
    return _combine(x, rw, pos, ys)


def _trunk_layer(x, layer, tables, lam, lam_init, attend, conv_state, nseq, p):
    a = p["attn_width"]
    ch = p["conv_ch"]
    h = _rmsnorm(x, p["norm1_g"][layer])
    q, = _qk_proj(h, p["w_in"], layer, 0, a, p["q_norm_g"][layer], tables, HEAD_DIM ** -0.5, (BF16,))
    k32, k16 = _qk_proj(h, p["w_in"], layer, a, a, p["k_norm_g"][layer], tables, 1.0, (F32, BF16))
    v32, v16 = _v_proj(h, p["w_in"], layer, 2 * a, a)
    u = _glu_proj(h, p["w_in"], layer, 3 * a, 3 * a + ch, ch)
    o = attend(q, k16, v16, lam, p["subln_g"][layer], 1.0 - lam_init)
    c = _conv_module(u, conv_state, nseq, p["conv_w"][layer], p["conv_b"][layer], p["conv_ln_g"][layer],
                     p["conv_ln_b"][layer], p["w_pw2"], layer)
    x = _out_proj(x, o, c, p["w_out"], layer)
    x = _moe_block(x, p["norm2_g"][layer], p["w_route"][layer], p["b_route"][layer], p["w1"], p["w3"], p["w2"],
                   layer, p["n_experts"], p["n_groups"])
    seq_len = u.shape[0] // nseq
    assert seq_len >= CONV_STATE
    new_conv = u.reshape(nseq, seq_len, ch)[:, seq_len - CONV_STATE:]
    return x, k32, v32, new_conv


def kernel(x_prompt, x_sample, cache_k, cache_v, state_conv, norm1_g, w_in, q_norm_g, k_norm_g, lambda_q1, lambda_k1, lambda_q2, lambda_k2, subln_g, conv_w, conv_b, conv_ln_g, conv_ln_b, w_pw2, w_out, norm2_g, w_group_router, b_group_router, w_expert_router, b_expert_router, w1, w3, w2):
    depth, d, _ = w_in.shape
    nbp, seq, _ = x_prompt.shape
    nbs, dec_seq, _ = x_sample.shape
    past, nh, qk = cache_k.shape[2], cache_k.shape[3], cache_k.shape[4]
    a = nh * qk
    ch = state_conv.shape[-1]
    n_groups = w_group_router.shape[-1]
    n_experts = w_expert_router.shape[-1]
    assert nbp == 1 and qk == 2 * HEAD_DIM and n_experts + n_groups <= ROUTE_LANES

    pad = jnp.zeros((depth, d, ROUTE_LANES - n_experts - n_groups), F32)
    p = {
        "attn_width": a, "conv_ch": ch, "n_experts": n_experts, "n_groups": n_groups,
        "norm1_g": norm1_g, "q_norm_g": q_norm_g, "k_norm_g": k_norm_g, "subln_g": subln_g,
        "conv_w": conv_w, "conv_b": conv_b, "conv_ln_g": conv_ln_g, "conv_ln_b": conv_ln_b, "norm2_g": norm2_g,
        "w_in": w_in.astype(BF16), "w_pw2": w_pw2.astype(BF16), "w_out": w_out.astype(BF16),
        "w1": w1.astype(BF16), "w3": w3.astype(BF16), "w2": w2.astype(BF16),
        "w_route": jnp.concatenate([w_expert_router, w_group_router, pad], axis=-1),
        "b_route": jnp.concatenate([b_expert_router, b_group_router, pad[:, 0]], axis=-1)[:, None, :],
    }
    tables_p = _rope_tables(jnp.arange(seq))
    tables_s = _rope_tables(jnp.tile(past + jnp.arange(dec_seq), nbs))
    conv_zero = jnp.zeros((nbp, CONV_STATE, ch), F32)
    cache_k2 = cache_k.reshape(depth, nbs, past, a)
    cache_v2 = cache_v.reshape(depth, nbs, past, a)

    yp = x_prompt.reshape(nbp * seq, d)
    ys = x_sample.reshape(nbs * dec_seq, d)
    outs = [[] for _ in range(6)]
    for l in range(depth):
        lam_init = 0.8 - 0.6 * math.exp(-0.3 * l)
        lam = (jnp.exp(jnp.sum(lambda_q1[l] * lambda_k1[l])) - jnp.exp(jnp.sum(lambda_q2[l] * lambda_k2[l]))
               + lam_init).reshape(1).astype(F32)
        attend_s = functools.partial(_attn_sample_call, cache_k2, cache_v2, l)
        yp, kp, vp, cp = _trunk_layer(yp, l, tables_p, lam, lam_init, _attn_prompt, conv_zero, nbp, p)
        ys, kn, vn, cn = _trunk_layer(ys, l, tables_s, lam, lam_init, attend_s, state_conv[l], nbs, p)
        for lst, val in zip(outs, (kp, vp, cp, kn, vn, cn)):
            lst.append(val)
    kp, vp, cp, kn, vn, cn = (jnp.stack(lst) for lst in outs)
    return (yp.reshape(nbp, seq, d), ys.reshape(nbs, dec_seq, d),
            kp.reshape(depth, nbp, seq, nh, qk), vp.reshape(depth, nbp, seq, nh, qk), cp,
            kn.reshape(depth, nbs, dec_seq, nh, qk), vn.reshape(depth, nbs, dec_seq, nh, qk), cn)


def _attn_sample_call(cache_k, cache_v, layer, q, k, v, lam, subln_g, out_scale):
    return _attn_sample(q, k, v, cache_k, cache_v, layer, lam, subln_g, out_scale)
```

```python
import functools
import math

import jax
import jax.numpy as jnp
from jax import lax
from jax.experimental import pallas as pl
from jax.experimental.pallas import tpu as pltpu

F32 = jnp.float32
BF16 = jnp.bfloat16

HEAD_DIM = 128
ROT_DIM = HEAD_DIM // 4
ROPE_THETA = 500000.0
CHUNK = 64
CONV_WIDTH = 31
CONV_STATE = CONV_WIDTH - 1
TOP_K_IN_GROUP = 2
EPS = 1e-6
NEG_INF = -1e30

LANES = 128
SUBLANES = 8
VMEM_LIMIT_BYTES = 56 * 1024 * 1024

CONV_HALO = 32
ROUTE_LANES = LANES


def _params(*semantics):
    return pltpu.CompilerParams(dimension_semantics=semantics, vmem_limit_bytes=VMEM_LIMIT_BYTES)


def _row_tile(t, cap):
    tile = min(t, cap)
    assert t % tile == 0, (t, tile)
    return tile


def _rmsnorm_kernel(x_ref, g_ref, o_ref):
    x = x_ref[...]
    ms = jnp.mean(x * x, axis=-1, keepdims=True)
    o_ref[...] = (x * lax.rsqrt(ms + EPS) * g_ref[...]).astype(o_ref.dtype)


def _rmsnorm(x, g):
    t, d = x.shape
    tm = _row_tile(t, 256)
    return pl.pallas_call(
        _rmsnorm_kernel,
        grid=(t // tm,),
        in_specs=[pl.BlockSpec((tm, d), lambda i: (i, 0)), pl.BlockSpec((1, d), lambda i: (0, 0))],
        out_specs=pl.BlockSpec((tm, d), lambda i: (i, 0)),
        out_shape=jax.ShapeDtypeStruct((t, d), BF16),
        compiler_params=_params("parallel"),
        name="rmsnorm",
    )(x, g.reshape(1, d))


def _rope_tables(pos):
    half = ROT_DIM // 2
    inv_freq = 1.0 / (ROPE_THETA ** (jnp.arange(half, dtype=F32) * 2.0 / ROT_DIM))
    ang = pos.astype(F32)[:, None] * inv_freq[None, :]
    cos, sin = jnp.cos(ang), jnp.sin(ang)
    t = pos.shape[0]
    c = jnp.concatenate([cos, cos, jnp.ones((t, HEAD_DIM - ROT_DIM), F32)], axis=-1)
    s = jnp.concatenate([-sin, sin, jnp.zeros((t, HEAD_DIM - ROT_DIM), F32)], axis=-1)
    return c, s


QK_PROJ_COLS = 256


def _qk_proj_kernel(h_ref, w_ref, g_ref, c_ref, s_ref, *out_refs, scale):
    h = h_ref[...]
    g, c, sn = g_ref[...], c_ref[...], s_ref[...]
    first_half = lax.broadcasted_iota(jnp.int32, c.shape, 1) < ROT_DIM // 2
    for q in range(w_ref.shape[1] // QK_PROJ_COLS):
        acc = jnp.dot(h, w_ref[:, q * QK_PROJ_COLS:(q + 1) * QK_PROJ_COLS], preferred_element_type=F32)
        for j in range(QK_PROJ_COLS // HEAD_DIM):
            x = acc[:, j * HEAD_DIM:(j + 1) * HEAD_DIM]
            ms = jnp.mean(x * x, axis=-1, keepdims=True)
            y = x * lax.rsqrt(ms + EPS) * g
            partner = jnp.where(first_half, pltpu.roll(y, HEAD_DIM - ROT_DIM // 2, 1),
                                pltpu.roll(y, ROT_DIM // 2, 1))
            r = y * c + partner * sn
            if scale != 1.0:
                r = r * scale
            cols = slice(q * QK_PROJ_COLS + j * HEAD_DIM, q * QK_PROJ_COLS + (j + 1) * HEAD_DIM)
            for o_ref in out_refs:
                o_ref[:, cols] = r.astype(o_ref.dtype)


def _qk_proj(h, w_bf, layer, col0, width, gain, tables, scale, out_dtypes):
    t, d = h.shape
    tm = _row_tile(t, 512)
    tn = min(width, 1024)
    assert width % tn == 0 and col0 % tn == 0 and tn % QK_PROJ_COLS == 0
    jb = col0 // tn
    c, s = tables
    tab_spec = pl.BlockSpec((tm, HEAD_DIM), lambda i, j: (i, 0))
    return pl.pallas_call(
        functools.partial(_qk_proj_kernel, scale=scale),
        grid=(t // tm, width // tn),
        in_specs=[
            pl.BlockSpec((tm, d), lambda i, j: (i, 0)),
            pl.BlockSpec((None, d, tn), lambda i, j: (layer, 0, jb + j)),
            pl.BlockSpec((1, HEAD_DIM), lambda i, j: (0, 0)),
            tab_spec, tab_spec,
        ],
        out_specs=[pl.BlockSpec((tm, tn), lambda i, j: (i, j)) for _ in out_dtypes],
        out_shape=[jax.ShapeDtypeStruct((t, width), dt) for dt in out_dtypes],
        compiler_params=_params("parallel", "parallel"),
        name="qk_proj",
    )(h, w_bf, gain.reshape(1, HEAD_DIM), c, s)


def _v_proj_kernel(h_ref, w_ref, o32_ref, o16_ref):
    acc = jnp.dot(h_ref[...], w_ref[...], preferred_element_type=F32)
    o32_ref[...] = acc
    o16_ref[...] = acc.astype(o16_ref.dtype)


def _v_proj(h, w_bf, layer, col0, width):
    t, d = h.shape
    tm = _row_tile(t, 1024)
    tn = min(width, 512)
    assert width % tn == 0 and col0 % tn == 0
    jb = col0 // tn
    return pl.pallas_call(
        _v_proj_kernel,
        grid=(t // tm, width // tn),
        in_specs=[
            pl.BlockSpec((tm, d), lambda i, j: (i, 0)),
            pl.BlockSpec((None, d, tn), lambda i, j: (layer, 0, jb + j)),
        ],
        out_specs=[pl.BlockSpec((tm, tn), lambda i, j: (i, j))] * 2,
        out_shape=[jax.ShapeDtypeStruct((t, width), F32), jax.ShapeDtypeStruct((t, width), BF16)],
        compiler_params=_params("parallel", "parallel"),
        name="v_proj",
    )(h, w_bf)


def _glu_proj_kernel(h_ref, wa_ref, wb_ref, u_ref):
    h = h_ref[...]
    a = jnp.dot(h, wa_ref[...], preferred_element_type=F32)
    b = jnp.dot(h, wb_ref[...], preferred_element_type=F32)
    u_ref[...] = a * jax.nn.sigmoid(b)


def _glu_proj(h, w_bf, layer, col_a, col_b, width):
    t, d = h.shape
    tm = _row_tile(t, 1024)
    tn = min(width, 512)
    assert width % tn == 0 and col_a % tn == 0 and col_b % tn == 0
    ja, jb = col_a // tn, col_b // tn
    return pl.pallas_call(
        _glu_proj_kernel,
        grid=(t // tm, width // tn),
        in_specs=[
            pl.BlockSpec((tm, d), lambda i, j: (i, 0)),
            pl.BlockSpec((None, d, tn), lambda i, j: (layer, 0, ja + j)),
            pl.BlockSpec((None, d, tn), lambda i, j: (layer, 0, jb + j)),
        ],
        out_specs=pl.BlockSpec((tm, tn), lambda i, j: (i, j)),
        out_shape=jax.ShapeDtypeStruct((t, width), F32),
        compiler_params=_params("parallel", "parallel"),
        name="glu_proj",
    )(h, w_bf, w_bf)


Q_SCALE = HEAD_DIM ** -0.5 * math.log2(math.e)
ATTN_BLOCK = 1024
ATTN_ROW_CHUNK = 512


def _stack_components(qs_ref, q, tq):
    z = jnp.zeros((tq, HEAD_DIM), q.dtype)
    qs_ref[0:tq, 0:HEAD_DIM] = q[:, 0:HEAD_DIM]
    qs_ref[0:tq, HEAD_DIM:] = z
    qs_ref[tq:, 0:HEAD_DIM] = z
    qs_ref[tq:, HEAD_DIM:] = q[:, HEAD_DIM:]


def _lane_repeat(x, n):
    return x if n == 1 else jnp.concatenate([x] * n, axis=1)


def _scores(qs, k):
    return lax.dot_general(qs, k, (((1,), (1,)), ((), ())), preferred_element_type=F32)


def _diff_subnorm(o1, o2, lam, g, out_scale):
    o = o1 - lam * o2
    ms = jnp.mean(o * o, axis=-1, keepdims=True)
    return o * lax.rsqrt(ms + EPS) * g * out_scale


def _attn_prompt_kernel(lam_ref, q_ref, k_ref, v_ref, g_ref, o_ref, qs_ref, s_ref, p_ref, mc_ref, m_ref, l_ref, acc_ref, *, tq,
                        rc,
                        out_scale):
    qi = pl.program_id(1)
    _stack_components(qs_ref, q_ref[...], tq)
    m_ref[...] = jnp.full(m_ref.shape, NEG_INF, F32)
    l_ref[...] = jnp.zeros(l_ref.shape, F32)
    acc_ref[...] = jnp.zeros(acc_ref.shape, F32)
    n_chunks = 2 * tq // rc

    def key_block(ref, ki):
        return ref[pl.ds(pl.multiple_of(ki * tq, tq), tq), :]

    def consume(c, v, masked):
        rows = slice(c * rc, (c + 1) * rc)
        s = s_ref[rows, :]
        if masked:
            row = lax.broadcasted_iota(jnp.int32, s.shape, 0) + (c * rc) % tq
            col = lax.broadcasted_iota(jnp.int32, s.shape, 1)
            s = jnp.where((col // CHUNK) <= (row // CHUNK), s, NEG_INF)
            m_cur = jnp.max(s, axis=1, keepdims=True)
        else:
            m_cur = mc_ref[rows, :]
        m_prev = m_ref[rows, :]
        m_new = jnp.maximum(m_prev, m_cur)
        alpha = jnp.exp2(m_prev - m_new)
        p = jnp.exp2(s - _lane_repeat(m_new, tq // LANES))
        l_ref[rows, :] = alpha * l_ref[rows, :] + jnp.sum(p, axis=1, keepdims=True)
        p_ref[rows, :] = p.astype(p_ref.dtype)
        pv = jnp.dot(p_ref[rows, :], v, preferred_element_type=F32)
        acc_ref[rows, :] = _lane_repeat(alpha, acc_ref.shape[1] // LANES) * acc_ref[rows, :] + pv
        m_ref[rows, :] = m_new

    def produce(c, k):
        rows = slice(c * rc, (c + 1) * rc)
        s = _scores(qs_ref[rows, :], k)
        s_ref[rows, :] = s
        mc_ref[rows, :] = jnp.broadcast_to(jnp.max(s, axis=1, keepdims=True), (rc, LANES))

    k0 = key_block(k_ref, 0)
    for c in range(n_chunks):
        produce(c, k0)

    def body(ki, carry):
        v = key_block(v_ref, ki)
        k_next = key_block(k_ref, ki + 1)
        for c in range(n_chunks):
            consume(c, v, False)
            produce(c, k_next)
        return carry

    lax.fori_loop(0, qi, body, 0)
    v = key_block(v_ref, qi)
    for c in range(n_chunks):
        consume(c, v, True)

    o = acc_ref[...] / _lane_repeat(l_ref[...], acc_ref.shape[1] // LANES)
    o_ref[...] = _diff_subnorm(o[0:tq], o[tq:], lam_ref[0], g_ref[...], out_scale).astype(o_ref.dtype)


def _attn_prompt(q, k, v, lam, subln_g, out_scale):
    t, a = q.shape
    qk = 2 * HEAD_DIM
    nh = a // qk
    tq = _row_tile(t, ATTN_BLOCK)
    assert tq % CHUNK == 0
    return pl.pallas_call(
        functools.partial(_attn_prompt_kernel, tq=tq, rc=min(tq, ATTN_ROW_CHUNK), out_scale=out_scale),
        grid=(nh, t // tq),
        in_specs=[
            pl.BlockSpec(memory_space=pltpu.SMEM),
            pl.BlockSpec((tq, qk), lambda h, i: (i, h)),
            pl.BlockSpec((t, qk), lambda h, i: (0, h), pipeline_mode=pl.Buffered(1)),
            pl.BlockSpec((t, qk), lambda h, i: (0, h), pipeline_mode=pl.Buffered(1)),
            pl.BlockSpec((1, qk), lambda h, i: (0, 0)),
        ],
        out_specs=pl.BlockSpec((tq, qk), lambda h, i: (i, h)),
        out_shape=jax.ShapeDtypeStruct((t, a), BF16),
        scratch_shapes=[
            pltpu.VMEM((2 * tq, qk), BF16),
            pltpu.VMEM((2 * tq, tq), F32),
            pltpu.VMEM((2 * tq, tq), BF16),
            pltpu.VMEM((2 * tq, LANES), F32),
            pltpu.VMEM((2 * tq, LANES), F32),
            pltpu.VMEM((2 * tq, LANES), F32),
            pltpu.VMEM((2 * tq, qk), F32),
        ],
        compiler_params=_params("parallel", "arbitrary"),
        name="attn_prompt",
    )(lam, q, k, v, subln_g.reshape(1, qk))


def _attn_sample_kernel(lam_ref, q_ref, kn_ref, vn_ref, kc_ref, vc_ref, g_ref, o_ref, qs_ref, *, ts, out_scale):
    _stack_components(qs_ref, q_ref[...], ts)
    qs = qs_ref[...]
    kc = kc_ref[...].astype(BF16)
    vc = vc_ref[...].astype(BF16)
    s_c = _scores(qs, kc)
    s_n = _scores(qs, kn_ref[...])
    m = jnp.maximum(jnp.max(s_c, axis=1, keepdims=True), jnp.max(s_n, axis=1, keepdims=True))
    p_c = jnp.exp2(s_c - m)
    p_n = jnp.exp2(s_n - m)
    l = jnp.sum(p_c, axis=1, keepdims=True) + jnp.sum(p_n, axis=1, keepdims=True)
    acc = jnp.dot(p_c.astype(BF16), vc, preferred_element_type=F32)
    acc = acc + jnp.dot(p_n.astype(BF16), vn_ref[...], preferred_element_type=F32)
    o = acc / l
    o_ref[...] = _diff_subnorm(o[0:ts], o[ts:], lam_ref[0], g_ref[...], out_scale).astype(o_ref.dtype)


def _attn_sample(q, k, v, cache_k, cache_v, layer, lam, subln_g, out_scale):
    _, nb, past, a = cache_k.shape
    ts = q.shape[0] // nb
    qk = 2 * HEAD_DIM
    nh = a // qk
    new_spec = pl.BlockSpec((ts, qk), lambda b, h: (b, h))
    cache_spec = pl.BlockSpec((None, None, past, qk), lambda b, h: (layer, b, 0, h))
    return pl.pallas_call(
        functools.partial(_attn_sample_kernel, ts=ts, out_scale=out_scale),
        grid=(nb, nh),
        in_specs=[pl.BlockSpec(memory_space=pltpu.SMEM), new_spec, new_spec, new_spec, cache_spec, cache_spec,
                  pl.BlockSpec((1, qk), lambda b, h: (0, 0))],
        out_specs=new_spec,
        out_shape=jax.ShapeDtypeStruct(q.shape, BF16),
        scratch_shapes=[pltpu.VMEM((2 * ts, qk), BF16)],
        compiler_params=_params("parallel", "parallel"),
        name="attn_sample",
    )(lam, q, k, v, cache_k, cache_v, subln_g.reshape(1, qk))


CONV_LANE_CHUNK = 128


def _conv_kernel(u_ref, prev_ref, state_ref, cw_ref, cb_ref, g_ref, b_ref, w_ref, o_ref, buf_ref, sh_ref, y_ref, *,
                 tm, tiles_per_seq):
    first = (pl.program_id(0) % tiles_per_seq) == 0
    lo = CONV_HALO - CONV_STATE

    @pl.when(first)
    def _():
        buf_ref[lo:CONV_HALO, :] = state_ref[...]

    @pl.when(jnp.logical_not(first))
    def _():
        buf_ref[lo:CONV_HALO, :] = prev_ref[lo:CONV_HALO, :]

    buf_ref[0:lo, :] = jnp.zeros((lo, buf_ref.shape[1]), F32)
    buf_ref[CONV_HALO:, :] = u_ref[...]
    nch = u_ref.shape[1]
    span = CONV_HALO + tm - SUBLANES

    def chunk(ci, carry):
        c0 = pl.multiple_of(ci * CONV_LANE_CHUNK, CONV_LANE_CHUNK)
        lanes = pl.ds(c0, CONV_LANE_CHUNK)
        for b in range(1, SUBLANES):
            sh_ref[b, 0:span, :] = buf_ref[b:b + span, lanes]
        acc = jnp.broadcast_to(cb_ref[:, lanes], (tm, CONV_LANE_CHUNK))
        for w in range(CONV_WIDTH):
            a, b = divmod(lo + w, SUBLANES)
            if b == 0:
                rows = buf_ref[SUBLANES * a:SUBLANES * a + tm, lanes]
            else:
                rows = sh_ref[b, SUBLANES * a:SUBLANES * a + tm, :]
            acc = acc + rows * cw_ref[w:w + 1, lanes]
        y_ref[:, lanes] = acc
        return carry

    lax.fori_loop(0, nch // CONV_LANE_CHUNK, chunk, 0)

    y = y_ref[...]
    mu = jnp.mean(y, axis=-1, keepdims=True)
    yc = y - mu
    var = jnp.mean(yc * yc, axis=-1, keepdims=True)
    yn = yc * lax.rsqrt(var + EPS) * g_ref[...] + b_ref[...]
    sw = yn * jax.nn.sigmoid(yn)
    o_ref[...] = jnp.dot(sw.astype(BF16), w_ref[...], preferred_element_type=F32).astype(o_ref.dtype)


def _conv_module(u, state, nseq, conv_w, conv_b, ln_g, ln_b, w_pw2_bf, layer):
    t, ch = u.shape
    seq_len = t // nseq
    tm = _row_tile(seq_len, 256)
    assert tm % CONV_HALO == 0 and ch % CONV_LANE_CHUNK == 0
    tiles_per_seq = seq_len // tm
    halo_blocks = tm // CONV_HALO
    vec = lambda x: x.reshape(1, ch)
    const = lambda i: (0, 0)
    return pl.pallas_call(
        functools.partial(_conv_kernel, tm=tm, tiles_per_seq=tiles_per_seq),
        grid=(t // tm,),
        in_specs=[
            pl.BlockSpec((tm, ch), lambda i: (i, 0)),
            pl.BlockSpec((CONV_HALO, ch), lambda i: (jnp.maximum(i * halo_blocks - 1, 0), 0)),
            pl.BlockSpec((None, CONV_STATE, ch), lambda i: (i // tiles_per_seq, 0, 0)),
            pl.BlockSpec((CONV_WIDTH, ch), const),
            pl.BlockSpec((1, ch), const), pl.BlockSpec((1, ch), const), pl.BlockSpec((1, ch), const),
            pl.BlockSpec((None, ch, ch), lambda i: (layer, 0, 0)),
        ],
        out_specs=pl.BlockSpec((tm, ch), lambda i: (i, 0)),
        out_shape=jax.ShapeDtypeStruct((t, ch), BF16),
        scratch_shapes=[pltpu.VMEM((CONV_HALO + tm, ch), F32),
                        pltpu.VMEM((SUBLANES, CONV_HALO + tm, CONV_LANE_CHUNK), F32),
                        pltpu.VMEM((tm, ch), F32)],
        compiler_params=_params("parallel"),
        name="conv_module",
    )(u, u, state, conv_w, vec(conv_b), vec(ln_g), vec(ln_b), w_pw2_bf)


def _out_proj_kernel(x_ref, o_ref, c_ref, wt_ref, wb_ref, y_ref):
    acc = jnp.dot(o_ref[...], wt_ref[...], preferred_element_type=F32)
    acc = acc + jnp.dot(c_ref[...], wb_ref[...], preferred_element_type=F32)
    y_ref[...] = x_ref[...] + acc


def _out_proj(x, o, c, w_out_bf, layer):
    t, d = x.shape
    a = o.shape[1]
    assert c.shape[1] == a and 2 * a == d
    tm = _row_tile(t, 1024)
    tn = min(d, 512)
    return pl.pallas_call(
        _out_proj_kernel,
        grid=(t // tm, d // tn),
        in_specs=[
            pl.BlockSpec((tm, tn), lambda i, j: (i, j)),
            pl.BlockSpec((tm, a), lambda i, j: (i, 0)),
            pl.BlockSpec((tm, a), lambda i, j: (i, 0)),
            pl.BlockSpec((None, a, tn), lambda i, j: (layer, 0, j)),
            pl.BlockSpec((None, a, tn), lambda i, j: (layer, 1, j)),
        ],
        out_specs=pl.BlockSpec((tm, tn), lambda i, j: (i, j)),
        out_shape=jax.ShapeDtypeStruct((t, d), F32),
        compiler_params=_params("parallel", "parallel"),
        name="out_proj",
    )(x, o, c, w_out_bf, w_out_bf)


def _router_kernel(x_ref, g_ref, wr_ref, br_ref, h_ref, ri_ref, rw_ref, cnt_ref, carry_ref, *, n_experts, n_groups,
                   per_group):
    step = pl.program_id(0)

    @pl.when(step == 0)
    def _():
        carry_ref[...] = jnp.zeros(carry_ref.shape, F32)

    x = x_ref[...]
    ms = jnp.mean(x * x, axis=-1, keepdims=True)
    h = x * lax.rsqrt(ms + EPS) * g_ref[...]
    h_ref[...] = h
    logits = jnp.dot(h, wr_ref[...], preferred_element_type=F32, precision=lax.Precision.HIGHEST) + br_ref[...]

    tm = x.shape[0]
    lane = lax.broadcasted_iota(jnp.int32, (tm, ROUTE_LANES), 1)
    big = jnp.int32(ROUTE_LANES)
    is_group = (lane >= n_experts) & (lane < n_experts + n_groups)
    gl = jnp.where(is_group, logits, -jnp.inf)
    gmax = jnp.max(gl, axis=1, keepdims=True)
    gidx = jnp.min(jnp.where(gl == gmax, lane, big), axis=1, keepdims=True) - n_experts
    g_gate = 1.0 / jnp.sum(jnp.exp(gl - gmax), axis=1, keepdims=True)

    in_group = (lane < n_experts) & ((lane // per_group) == gidx)
    e_in = jnp.where(in_group, logits, -jnp.inf)
    m1 = jnp.max(e_in, axis=1, keepdims=True)
    i1 = jnp.min(jnp.where(e_in == m1, lane, big), axis=1, keepdims=True)
    e_in2 = jnp.where(lane == i1, -jnp.inf, e_in)
    m2 = jnp.max(e_in2, axis=1, keepdims=True)
    i2 = jnp.min(jnp.where(e_in2 == m2, lane, big), axis=1, keepdims=True)
    t = jnp.exp(m2 - m1)
    w1 = g_gate / (1.0 + t)
    w2 = g_gate * t / (1.0 + t)

    sel1 = lane == i1
    sel2 = lane == i2
    onehot = jnp.where(sel1 | sel2, 1.0, 0.0).astype(BF16)
    r = lax.broadcasted_iota(jnp.int32, (tm, tm), 0)
    c = lax.broadcasted_iota(jnp.int32, (tm, tm), 1)
    lower = jnp.where(c < r, 1.0, 0.0).astype(BF16)
    before = jnp.dot(lower, onehot, preferred_element_type=F32) + carry_ref[...]
    rank1 = jnp.sum(jnp.where(sel1, before, 0.0), axis=1, keepdims=True).astype(jnp.int32)
    rank2 = jnp.sum(jnp.where(sel2, before, 0.0), axis=1, keepdims=True).astype(jnp.int32)
    carry_ref[...] = carry_ref[...] + jnp.sum(onehot.astype(F32), axis=0, keepdims=True)
    cnt_ref[...] = carry_ref[...].astype(jnp.int32)

    ri = jnp.where(lane == 0, i1, jnp.where(lane == 1, i2, jnp.where(lane == 2, rank1,
                                                                     jnp.where(lane == 3, rank2, 0))))
    ri_ref[...] = ri
    rw_ref[...] = jnp.where(lane == 0, w1, jnp.where(lane == 1, w2, 0.0))


def _router(x, norm_g, w_route, b_route, n_experts, n_groups):
    t, d = x.shape
    tm = _row_tile(t, 256)
    const = lambda i: (0, 0)
    return pl.pallas_call(
        functools.partial(_router_kernel, n_experts=n_experts, n_groups=n_groups,
                          per_group=n_experts // n_groups),
        grid=(t // tm,),
        in_specs=[
            pl.BlockSpec((tm, d), lambda i: (i, 0)),
            pl.BlockSpec((1, d), const),
            pl.BlockSpec((d, ROUTE_LANES), const),
            pl.BlockSpec((1, ROUTE_LANES), const),
        ],
        out_specs=[
            pl.BlockSpec((tm, d), lambda i: (i, 0)),
            pl.BlockSpec((tm, ROUTE_LANES), lambda i: (i, 0)),
            pl.BlockSpec((tm, ROUTE_LANES), lambda i: (i, 0)),
            pl.BlockSpec((1, ROUTE_LANES), const),
        ],
        out_shape=[
            jax.ShapeDtypeStruct((t, d), F32),
            jax.ShapeDtypeStruct((t, ROUTE_LANES), jnp.int32),
            jax.ShapeDtypeStruct((t, ROUTE_LANES), F32),
            jax.ShapeDtypeStruct((1, ROUTE_LANES), jnp.int32),
        ],
        scratch_shapes=[pltpu.VMEM((1, ROUTE_LANES), F32)],
        compiler_params=_params("arbitrary"),
        name="router",
    )(x, norm_g.reshape(1, d), w_route, b_route)


ROW_DMA_UNROLL = 8


def _row_copy(src_ref, src_row, dst_ref, dst_row, sem):
    return pltpu.make_async_copy(src_ref.at[pl.ds(src_row, 1)], dst_ref.at[pl.ds(dst_row, 1)], sem)


def _dispatch_kernel(pos_ref, h_ref, xs_in_ref, xs_ref, sem, *, tb):
    del xs_in_ref
    n = TOP_K_IN_GROUP * tb

    def start(r, carry):
        _row_copy(h_ref, r // TOP_K_IN_GROUP, xs_ref, pos_ref[0, 0, r], sem).start()
        return carry

    def wait(r, carry):
        _row_copy(h_ref, r // TOP_K_IN_GROUP, xs_ref, pos_ref[0, 0, r], sem).wait()
        return carry

    lax.fori_loop(0, n, start, 0, unroll=ROW_DMA_UNROLL)
    lax.fori_loop(0, n, wait, 0, unroll=ROW_DMA_UNROLL)


def _dispatch(h, pos, n_slots):
    t, d = h.shape
    tb = _row_tile(t, 256)
    nb = t // tb
    xs0 = jnp.zeros((n_slots, d), h.dtype)
    return pl.pallas_call(
        functools.partial(_dispatch_kernel, tb=tb),
        grid=(nb,),
        in_specs=[
            pl.BlockSpec((1, 1, TOP_K_IN_GROUP * tb), lambda i: (i, 0, 0), memory_space=pltpu.SMEM),
            pl.BlockSpec((tb, d), lambda i: (i, 0)),
            pl.BlockSpec(memory_space=pl.ANY),
        ],
        out_specs=pl.BlockSpec(memory_space=pl.ANY),
        out_shape=jax.ShapeDtypeStruct((n_slots, d), h.dtype),
        scratch_shapes=[pltpu.SemaphoreType.DMA(())],
        input_output_aliases={2: 0},
        compiler_params=_params("arbitrary"),
        name="moe_dispatch",
    )(pos.reshape(nb, 1, TOP_K_IN_GROUP * tb), h, xs0)


def _experts_kernel(te_ref, tb_ref, na_ref, x_ref, w1_ref, w3_ref, w2_ref, y_ref):
    del te_ref, tb_ref
    active = pl.program_id(0) < na_ref[0]

    @pl.when(active)
    def _():
        x = x_ref[...].astype(BF16)
        a = jnp.dot(x, w1_ref[...], preferred_element_type=F32)
        b = jnp.dot(x, w3_ref[...], preferred_element_type=F32)
        he = (a * jax.nn.sigmoid(a)) * b
        y_ref[...] = jnp.dot(he.astype(BF16), w2_ref[...], preferred_element_type=F32)

    @pl.when(jnp.logical_not(active))
    def _():
        y_ref[...] = jnp.zeros(y_ref.shape, F32)


def _experts(xs, tile_expert, tile_block, n_active, w1_bf, w3_bf, w2_bf, layer, tm):
    n_slots, d = xs.shape
    ff = w1_bf.shape[-1]
    n_tiles = n_slots // tm
    row = lambda i, te, tb, na: (tb[i], 0)
    grid_spec = pltpu.PrefetchScalarGridSpec(
        num_scalar_prefetch=3,
        grid=(n_tiles,),
        in_specs=[
            pl.BlockSpec((tm, d), row),
            pl.BlockSpec((None, None, d, ff), lambda i, te, tb, na: (layer, te[i], 0, 0)),
            pl.BlockSpec((None, None, d, ff), lambda i, te, tb, na: (layer, te[i], 0, 0)),
            pl.BlockSpec((None, None, ff, d), lambda i, te, tb, na: (layer, te[i], 0, 0)),
        ],
        out_specs=pl.BlockSpec((tm, d), lambda i, te, tb, na: (i, 0)),
    )
    return pl.pallas_call(
        _experts_kernel,
        grid_spec=grid_spec,
        out_shape=jax.ShapeDtypeStruct((n_slots, d), F32),
        compiler_params=_params("arbitrary"),
        name="moe_experts",
    )(tile_expert, tile_block, n_active, xs, w1_bf, w3_bf, w2_bf)


def _combine_kernel(pos_ref, x_ref, rw_ref, ys_ref, o_ref, buf_ref, sem, *, tb):
    n = TOP_K_IN_GROUP * tb

    def copy(r):
        k = r % TOP_K_IN_GROUP
        return _row_copy(ys_ref, pos_ref[0, 0, r], buf_ref.at[k], r // TOP_K_IN_GROUP, sem)

    def start(r, carry):
        copy(r).start()
        return carry

    def wait(r, carry):
        copy(r).wait()
        return carry

    lax.fori_loop(0, n, start, 0, unroll=ROW_DMA_UNROLL)
    lax.fori_loop(0, n, wait, 0, unroll=ROW_DMA_UNROLL)
    rw = rw_ref[...]
    o_ref[...] = x_ref[...] + (rw[:, 0:1] * buf_ref[0] + rw[:, 1:2] * buf_ref[1])


def _combine(x, rw, pos, ys):
    t, d = x.shape
    tb = _row_tile(t, 256)
    nb = t // tb
    return pl.pallas_call(
        functools.partial(_combine_kernel, tb=tb),
        grid=(nb,),
        in_specs=[
            pl.BlockSpec((1, 1, TOP_K_IN_GROUP * tb), lambda i: (i, 0, 0), memory_space=pltpu.SMEM),
            pl.BlockSpec((tb, d), lambda i: (i, 0)),
            pl.BlockSpec((tb, ROUTE_LANES), lambda i: (i, 0)),
            pl.BlockSpec(memory_space=pl.ANY),
        ],
        out_specs=pl.BlockSpec((tb, d), lambda i: (i, 0)),
        out_shape=jax.ShapeDtypeStruct((t, d), F32),
        scratch_shapes=[pltpu.VMEM((TOP_K_IN_GROUP, tb, d), F32), pltpu.SemaphoreType.DMA(())],
        compiler_params=_params("arbitrary"),
        name="moe_combine",
    )(pos.reshape(nb, 1, TOP_K_IN_GROUP * tb), x, rw, ys)


def _moe_block(x, norm_g, w_route, b_route, w1_bf, w3_bf, w2_bf, layer, n_experts, n_groups):
    t, d = x.shape
    h, ri, rw, counts = _router(x, norm_g, w_route, b_route, n_experts, n_groups)

    tm = _row_tile(t, 256)
    counts = counts[0, :n_experts]
    padded = ((counts + tm - 1) // tm) * tm
    ends = jnp.cumsum(padded)
    offsets = ends - padded
    n_tiles = (TOP_K_IN_GROUP * t) // tm + n_experts
    n_active = (ends[-1] // tm).astype(jnp.int32)
    tile_ids = jnp.arange(n_tiles, dtype=jnp.int32)
    tile_block = jnp.minimum(tile_ids, n_active - 1)
    tile_expert = jnp.sum((ends[None, :] <= (tile_block * tm)[:, None]).astype(jnp.int32), axis=1)
    tile_expert = jnp.minimum(tile_expert, n_experts - 1)
    chosen = ri[:, 0:TOP_K_IN_GROUP]
    is_expert = chosen[:, :, None] == jnp.arange(n_experts, dtype=jnp.int32)
    pos = jnp.sum(jnp.where(is_expert, offsets, 0), axis=-1) + ri[:, TOP_K_IN_GROUP:2 * TOP_K_IN_GROUP]
    pos = pos.astype(jnp.int32)

    xs = _dispatch(h, pos, n_tiles * tm)
    ys = _experts(xs, tile_expert, tile_block, n_active.reshape(1), w1_bf, w3_bf, w2_bf, layer, tm)
    return _combine(x, rw, pos, ys)


def _trunk_layer(x, layer, tables, lam, lam_init, attend, conv_state, nseq, p):
    a = p["attn_width"]
    ch = p["conv_ch"]
    h = _rmsnorm(x, p["norm1_g"][layer])
    q, = _qk_proj(h, p["w_in"], layer, 0, a, p["q_norm_g"][layer], tables, Q_SCALE, (BF16,))
    k32, k16 = _qk_proj(h, p["w_in"], layer, a, a, p["k_norm_g"][layer], tables, 1.0, (F32, BF16))
    v32, v16 = _v_proj(h, p["w_in"], layer, 2 * a, a)
    u = _glu_proj(h, p["w_in"], layer, 3 * a, 3 * a + ch, ch)
    o = attend(q, k16, v16, lam, p["subln_g"][layer], 1.0 - lam_init)
    c = _conv_module(u, conv_state, nseq, p["conv_w"][layer], p["conv_b"][layer], p["conv_ln_g"][layer],
                     p["conv_ln_b"][layer], p["w_pw2"], layer)
    x = _out_proj(x, o, c, p["w_out"], layer)
    x = _moe_block(x, p["norm2_g"][layer], p["w_route"][layer], p["b_route"][layer], p["w1"], p["w3"], p["w2"],
                   layer, p["n_experts"], p["n_groups"])
    seq_len = u.shape[0] // nseq
    assert seq_len >= CONV_STATE
    new_conv = u.reshape(nseq, seq_len, ch)[:, seq_len - CONV_STATE:]
    return x, k32, v32, new_conv


def kernel(x_prompt, x_sample, cache_k, cache_v, state_conv, norm1_g, w_in, q_norm_g, k_norm_g, lambda_q1, lambda_k1, lambda_q2, lambda_k2, subln_g, conv_w, conv_b, conv_ln_g, conv_ln_b, w_pw2, w_out, norm2_g, w_group_router, b_group_router, w_expert_router, b_expert_router, w1, w3, w2):
    depth, d, _ = w_in.shape
    nbp, seq, _ = x_prompt.shape
    nbs, dec_seq, _ = x_sample.shape
    past, nh, qk = cache_k.shape[2], cache_k.shape[3], cache_k.shape[4]
    a = nh * qk
    ch = state_conv.shape[-1]
    n_groups = w_group_router.shape[-1]
    n_experts = w_expert_router.shape[-1]
    assert nbp == 1 and qk == 2 * HEAD_DIM and n_experts + n_groups <= ROUTE_LANES

    pad = jnp.zeros((depth, d, ROUTE_LANES - n_experts - n_groups), F32)
    p = {
        "attn_width": a, "conv_ch": ch, "n_experts": n_experts, "n_groups": n_groups,
        "norm1_g": norm1_g, "q_norm_g": q_norm_g, "k_norm_g": k_norm_g, "subln_g": subln_g,
        "conv_w": conv_w, "conv_b": conv_b, "conv_ln_g": conv_ln_g, "conv_ln_b": conv_ln_b, "norm2_g": norm2_g,
        "w_in": w_in.astype(BF16), "w_pw2": w_pw2.astype(BF16), "w_out": w_out.astype(BF16),
        "w1": w1.astype(BF16), "w3": w3.astype(BF16), "w2": w2.astype(BF16),
        "w_route": jnp.concatenate([w_expert_router, w_group_router, pad], axis=-1),
        "b_route": jnp.concatenate([b_expert_router, b_group_router, pad[:, 0]], axis=-1)[:, None, :],
    }
    tables_p = _rope_tables(jnp.arange(seq))
    tables_s = _rope_tables(jnp.tile(past + jnp.arange(dec_seq), nbs))
    conv_zero = jnp.zeros((nbp, CONV_STATE, ch), F32)
    cache_k2 = cache_k.reshape(depth, nbs, past, a)
    cache_v2 = cache_v.reshape(depth, nbs, past, a)

    yp = x_prompt.reshape(nbp * seq, d)
    ys = x_sample.reshape(nbs * dec_seq, d)
    outs = [[] for _ in range(6)]
    for l in range(depth):
        lam_init = 0.8 - 0.6 * math.exp(-0.3 * l)
        lam = (jnp.exp(jnp.sum(lambda_q1[l] * lambda_k1[l])) - jnp.exp(jnp.sum(lambda_q2[l] * lambda_k2[l]))
               + lam_init).reshape(1).astype(F32)
        attend_s = functools.partial(_attn_sample_call, cache_k2, cache_v2, l)
        yp, kp, vp, cp = _trunk_layer(yp, l, tables_p, lam, lam_init, _attn_prompt, conv_zero, nbp, p)
        ys, kn, vn, cn = _trunk_layer(ys, l, tables_s, lam, lam_init, attend_s, state_conv[l], nbs, p)
        for lst, val in zip(outs, (kp, vp, cp, kn, vn, cn)):
            lst.append(val)
    kp, vp, cp, kn, vn, cn = (jnp.stack(lst) for lst in outs)
    return (yp.reshape(nbp, seq, d), ys.reshape(nbs, dec_seq, d),
            kp.reshape(depth, nbp, seq, nh, qk), vp.reshape(depth, nbp, seq, nh, qk), cp,
            kn.reshape(depth, nbs, dec_seq, nh, qk), vn.reshape(depth, nbs, dec_seq, nh, qk), cn)


def _attn_sample_call(cache_k, cache_v, layer, q, k, v, lam, subln_g, out_scale):
    return _attn_sample(q, k, v, cache_k, cache_v, layer, lam, subln_g, out_scale)
```

```python
import functools
import math

import jax
import jax.numpy as jnp
from jax import lax
from jax.experimental import pallas as pl
from jax.experimental.pallas import tpu as pltpu

F32 = jnp.float32
BF16 = jnp.bfloat16

HEAD_DIM = 128
ROT_DIM = HEAD_DIM // 4
ROPE_THETA = 500000.0
CHUNK = 64
CONV_WIDTH = 31
CONV_STATE = CONV_WIDTH - 1
TOP_K_IN_GROUP = 2
EPS = 1e-6
NEG_INF = -1e30

LANES = 128
SUBLANES = 8
VMEM_LIMIT_BYTES = 56 * 1024 * 1024

CONV_HALO = 32
ROUTE_LANES = LANES


def _params(*semantics):
    return pltpu.CompilerParams(dimension_semantics=semantics, vmem_limit_bytes=VMEM_LIMIT_BYTES)


def _row_tile(t, cap):
    tile = min(t, cap)
    assert t % tile == 0, (t, tile)
    return tile


def _rmsnorm_kernel(x_ref, g_ref, o_ref):
    x = x_ref[...]
    ms = jnp.mean(x * x, axis=-1, keepdims=True)
    o_ref[...] = (x * lax.rsqrt(ms + EPS) * g_ref[...]).astype(o_ref.dtype)


def _rmsnorm(x, g):
    t, d = x.shape
    tm = _row_tile(t, 256)
    return pl.pallas_call(
        _rmsnorm_kernel,
        grid=(t // tm,),
        in_specs=[pl.BlockSpec((tm, d), lambda i: (i, 0)), pl.BlockSpec((1, d), lambda i: (0, 0))],
        out_specs=pl.BlockSpec((tm, d), lambda i: (i, 0)),
        out_shape=jax.ShapeDtypeStruct((t, d), BF16),
        compiler_params=_params("parallel"),
        name="rmsnorm",
    )(x, g.reshape(1, d))


def _rope_tables(pos):
    half = ROT_DIM // 2
    inv_freq = 1.0 / (ROPE_THETA ** (jnp.arange(half, dtype=F32) * 2.0 / ROT_DIM))
    ang = pos.astype(F32)[:, None] * inv_freq[None, :]
    cos, sin = jnp.cos(ang), jnp.sin(ang)
    t = pos.shape[0]
    c = jnp.concatenate([cos, cos, jnp.ones((t, HEAD_DIM - ROT_DIM), F32)], axis=-1)
    s = jnp.concatenate([-sin, sin, jnp.zeros((t, HEAD_DIM - ROT_DIM), F32)], axis=-1)
    return c, s


QK_PROJ_COLS = 256


def _qk_proj_kernel(h_ref, w_ref, g_ref, c_ref, s_ref, *out_refs, scale):
    h = h_ref[...]
    g, c, sn = g_ref[...], c_ref[...], s_ref[...]
    first_half = lax.broadcasted_iota(jnp.int32, c.shape, 1) < ROT_DIM // 2
    for q in range(w_ref.shape[1] // QK_PROJ_COLS):
        acc = jnp.dot(h, w_ref[:, q * QK_PROJ_COLS:(q + 1) * QK_PROJ_COLS], preferred_element_type=F32)
        for j in range(QK_PROJ_COLS // HEAD_DIM):
            x = acc[:, j * HEAD_DIM:(j + 1) * HEAD_DIM]
            ms = jnp.mean(x * x, axis=-1, keepdims=True)
            y = x * lax.rsqrt(ms + EPS) * g
            partner = jnp.where(first_half, pltpu.roll(y, HEAD_DIM - ROT_DIM // 2, 1),
                                pltpu.roll(y, ROT_DIM // 2, 1))
            r = y * c + partner * sn
            if scale != 1.0:
                r = r * scale
            cols = slice(q * QK_PROJ_COLS + j * HEAD_DIM, q * QK_PROJ_COLS + (j + 1) * HEAD_DIM)
            for o_ref in out_refs:
                o_ref[:, cols] = r.astype(o_ref.dtype)


def _qk_proj(h, w_bf, layer, col0, width, gain, tables, scale, out_dtypes):
    t, d = h.shape
    tm = _row_tile(t, 512)
    tn = min(width, 1024)
    assert width % tn == 0 and col0 % tn == 0 and tn % QK_PROJ_COLS == 0
    jb = col0 // tn
    c, s = tables
    tab_spec = pl.BlockSpec((tm, HEAD_DIM), lambda i, j: (i, 0))
    return pl.pallas_call(
        functools.partial(_qk_proj_kernel, scale=scale),
        grid=(t // tm, width // tn),
        in_specs=[
            pl.BlockSpec((tm, d), lambda i, j: (i, 0)),
            pl.BlockSpec((None, d, tn), lambda i, j: (layer, 0, jb + j)),
            pl.BlockSpec((1, HEAD_DIM), lambda i, j: (0, 0)),
            tab_spec, tab_spec,
        ],
        out_specs=[pl.BlockSpec((tm, tn), lambda i, j: (i, j)) for _ in out_dtypes],
        out_shape=[jax.ShapeDtypeStruct((t, width), dt) for dt in out_dtypes],
        compiler_params=_params("parallel", "parallel"),
        name="qk_proj",
    )(h, w_bf, gain.reshape(1, HEAD_DIM), c, s)


def _v_proj_kernel(h_ref, w_ref, o32_ref, o16_ref):
    acc = jnp.dot(h_ref[...], w_ref[...], preferred_element_type=F32)
    o32_ref[...] = acc
    o16_ref[...] = acc.astype(o16_ref.dtype)


def _v_proj(h, w_bf, layer, col0, width):
    t, d = h.shape
    tm = _row_tile(t, 1024)
    tn = min(width, 512)
    assert width % tn == 0 and col0 % tn == 0
    jb = col0 // tn
    return pl.pallas_call(
        _v_proj_kernel,
        grid=(t // tm, width // tn),
        in_specs=[
            pl.BlockSpec((tm, d), lambda i, j: (i, 0)),
            pl.BlockSpec((None, d, tn), lambda i, j: (layer, 0, jb + j)),
        ],
        out_specs=[pl.BlockSpec((tm, tn), lambda i, j: (i, j))] * 2,
        out_shape=[jax.ShapeDtypeStruct((t, width), F32), jax.ShapeDtypeStruct((t, width), BF16)],
        compiler_params=_params("parallel", "parallel"),
        name="v_proj",
    )(h, w_bf)


def _glu_proj_kernel(h_ref, wa_ref, wb_ref, u_ref):
    h = h_ref[...]
    a = jnp.dot(h, wa_ref[...], preferred_element_type=F32)
    b = jnp.dot(h, wb_ref[...], preferred_element_type=F32)
    u_ref[...] = a * jax.nn.sigmoid(b)


def _glu_proj(h, w_bf, layer, col_a, col_b, width):
    t, d = h.shape
    tm = _row_tile(t, 1024)
    tn = min(width, 512)
    assert width % tn == 0 and col_a % tn == 0 and col_b % tn == 0
    ja, jb = col_a // tn, col_b // tn
    return pl.pallas_call(
        _glu_proj_kernel,
        grid=(t // tm, width // tn),
        in_specs=[
            pl.BlockSpec((tm, d), lambda i, j: (i, 0)),
            pl.BlockSpec((None, d, tn), lambda i, j: (layer, 0, ja + j)),
            pl.BlockSpec((None, d, tn), lambda i, j: (layer, 0, jb + j)),
        ],
        out_specs=pl.BlockSpec((tm, tn), lambda i, j: (i, j)),
        out_shape=jax.ShapeDtypeStruct((t, width), F32),
        compiler_params=_params("parallel", "parallel"),
        name="glu_proj",
    )(h, w_bf, w_bf)


Q_SCALE = HEAD_DIM ** -0.5 * math.log2(math.e)
ATTN_BLOCK = 1024
ATTN_ROW_CHUNK = 512


def _stack_components(qs_ref, q, tq):
    z = jnp.zeros((tq, HEAD_DIM), q.dtype)
    qs_ref[0:tq, 0:HEAD_DIM] = q[:, 0:HEAD_DIM]
    qs_ref[0:tq, HEAD_DIM:] = z
    qs_ref[tq:, 0:HEAD_DIM] = z
    qs_ref[tq:, HEAD_DIM:] = q[:, HEAD_DIM:]


def _lane_repeat(x, n):
    return x if n == 1 else jnp.concatenate([x] * n, axis=1)


def _scores(qs, k):
    return lax.dot_general(qs, k, (((1,), (1,)), ((), ())), preferred_element_type=F32)


def _diff_subnorm(o1, o2, lam, g, out_scale):
    o = o1 - lam * o2
    ms = jnp.mean(o * o, axis=-1, keepdims=True)
    return o * lax.rsqrt(ms + EPS) * g * out_scale


def _attn_prompt_kernel(lam_ref, q_ref, k_ref, v_ref, g_ref, o_ref, qs_ref, s_ref, p_ref, mc_ref, m_ref, l_ref, acc_ref, *, tq,
                        rc,
                        out_scale):
    qi = pl.program_id(1)
    _stack_components(qs_ref, q_ref[...], tq)
    m_ref[...] = jnp.full(m_ref.shape, NEG_INF, F32)
    l_ref[...] = jnp.zeros(l_ref.shape, F32)
    acc_ref[...] = jnp.zeros(acc_ref.shape, F32)
    n_chunks = 2 * tq // rc

    def key_block(ref, ki):
        return ref[pl.ds(pl.multiple_of(ki * tq, tq), tq), :]

    def consume(c, v, masked):
        rows = slice(c * rc, (c + 1) * rc)
        s = s_ref[rows, :]
        if masked:
            row = lax.broadcasted_iota(jnp.int32, s.shape, 0) + (c * rc) % tq
            col = lax.broadcasted_iota(jnp.int32, s.shape, 1)
            s = jnp.where((col // CHUNK) <= (row // CHUNK), s, NEG_INF)
            m_cur = jnp.max(s, axis=1, keepdims=True)
        else:
            m_cur = mc_ref[rows, :]
        m_prev = m_ref[rows, :]
        m_new = jnp.maximum(m_prev, m_cur)
        alpha = jnp.exp2(m_prev - m_new)
        p = jnp.exp2(s - _lane_repeat(m_new, tq // LANES))
        l_ref[rows, :] = alpha * l_ref[rows, :] + jnp.sum(p, axis=1, keepdims=True)
        p_ref[rows, :] = p.astype(p_ref.dtype)
        pv = jnp.dot(p_ref[rows, :], v, preferred_element_type=F32)
        acc_ref[rows, :] = _lane_repeat(alpha, acc_ref.shape[1] // LANES) * acc_ref[rows, :] + pv
        m_ref[rows, :] = m_new

    def produce(c, k):
        rows = slice(c * rc, (c + 1) * rc)
        s = _scores(qs_ref[rows, :], k)
        s_ref[rows, :] = s
        mc_ref[rows, :] = jnp.broadcast_to(jnp.max(s, axis=1, keepdims=True), (rc, LANES))

    k0 = key_block(k_ref, 0)
    for c in range(n_chunks):
        produce(c, k0)

    def body(ki, carry):
        v = key_block(v_ref, ki)
        k_next = key_block(k_ref, ki + 1)
        for c in range(n_chunks):
            consume(c, v, False)
            produce(c, k_next)
        return carry

    lax.fori_loop(0, qi, body, 0)
    v = key_block(v_ref, qi)
    for c in range(n_chunks):
        consume(c, v, True)

    o = acc_ref[...] / _lane_repeat(l_ref[...], acc_ref.shape[1] // LANES)
    o_ref[...] = _diff_subnorm(o[0:tq], o[tq:], lam_ref[0], g_ref[...], out_scale).astype(o_ref.dtype)


def _attn_prompt(q, k, v, lam, subln_g, out_scale):
    t, a = q.shape
    qk = 2 * HEAD_DIM
    nh = a // qk
    tq = _row_tile(t, ATTN_BLOCK)
    assert tq % CHUNK == 0
    return pl.pallas_call(
        functools.partial(_attn_prompt_kernel, tq=tq, rc=min(tq, ATTN_ROW_CHUNK), out_scale=out_scale),
        grid=(nh, t // tq),
        in_specs=[
            pl.BlockSpec(memory_space=pltpu.SMEM),
            pl.BlockSpec((tq, qk), lambda h, i: (i, h)),
            pl.BlockSpec((t, qk), lambda h, i: (0, h), pipeline_mode=pl.Buffered(1)),
            pl.BlockSpec((t, qk), lambda h, i: (0, h), pipeline_mode=pl.Buffered(1)),
            pl.BlockSpec((1, qk), lambda h, i: (0, 0)),
        ],
        out_specs=pl.BlockSpec((tq, qk), lambda h, i: (i, h)),
        out_shape=jax.ShapeDtypeStruct((t, a), BF16),
        scratch_shapes=[
            pltpu.VMEM((2 * tq, qk), BF16),
            pltpu.VMEM((2 * tq, tq), F32),
            pltpu.VMEM((2 * tq, tq), BF16),
            pltpu.VMEM((2 * tq, LANES), F32),
            pltpu.VMEM((2 * tq, LANES), F32),
            pltpu.VMEM((2 * tq, LANES), F32),
            pltpu.VMEM((2 * tq, qk), F32),
        ],
        compiler_params=_params("parallel", "arbitrary"),
        name="attn_prompt",
    )(lam, q, k, v, subln_g.reshape(1, qk))


def _attn_sample_kernel(lam_ref, q_ref, kn_ref, vn_ref, kc_ref, vc_ref, g_ref, o_ref, qs_ref, *, ts, out_scale):
    _stack_components(qs_ref, q_ref[...], ts)
    qs = qs_ref[...]
    kc = kc_ref[...].astype(BF16)
    vc = vc_ref[...].astype(BF16)
    s_c = _scores(qs, kc)
    s_n = _scores(qs, kn_ref[...])
    m = jnp.maximum(jnp.max(s_c, axis=1, keepdims=True), jnp.max(s_n, axis=1, keepdims=True))
    p_c = jnp.exp2(s_c - m)
    p_n = jnp.exp2(s_n - m)
    l = jnp.sum(p_c, axis=1, keepdims=True) + jnp.sum(p_n, axis=1, keepdims=True)
    acc = jnp.dot(p_c.astype(BF16), vc, preferred_element_type=F32)
    acc = acc + jnp.dot(p_n.astype(BF16), vn_ref[...], preferred_element_type=F32)
    o = acc / l
    o_ref[...] = _diff_subnorm(o[0:ts], o[ts:], lam_ref[0], g_ref[...], out_scale).astype(o_ref.dtype)


def _attn_sample(q, k, v, cache_k, cache_v, layer, lam, subln_g, out_scale):
    _, nb, past, a = cache_k.shape
    ts = q.shape[0] // nb
    qk = 2 * HEAD_DIM
    nh = a // qk
    new_spec = pl.BlockSpec((ts, qk), lambda b, h: (b, h))
    cache_spec = pl.BlockSpec((None, None, past, qk), lambda b, h: (layer, b, 0, h))
    return pl.pallas_call(
        functools.partial(_attn_sample_kernel, ts=ts, out_scale=out_scale),
        grid=(nb, nh),
        in_specs=[pl.BlockSpec(memory_space=pltpu.SMEM), new_spec, new_spec, new_spec, cache_spec, cache_spec,
                  pl.BlockSpec((1, qk), lambda b, h: (0, 0))],
        out_specs=new_spec,
        out_shape=jax.ShapeDtypeStruct(q.shape, BF16),
        scratch_shapes=[pltpu.VMEM((2 * ts, qk), BF16)],
        compiler_params=_params("parallel", "parallel"),
        name="attn_sample",
    )(lam, q, k, v, cache_k, cache_v, subln_g.reshape(1, qk))


CONV_LANE_CHUNK = 128


def _conv_kernel(u_ref, prev_ref, state_ref, cw_ref, cb_ref, g_ref, b_ref, w_ref, o_ref, buf_ref, sh_ref, y_ref, *,
                 tm, tiles_per_seq):
    first = (pl.program_id(0) % tiles_per_seq) == 0
    lo = CONV_HALO - CONV_STATE

    @pl.when(first)
    def _():
        buf_ref[lo:CONV_HALO, :] = state_ref[...]

    @pl.when(jnp.logical_not(first))
    def _():
        buf_ref[lo:CONV_HALO, :] = prev_ref[lo:CONV_HALO, :]

    buf_ref[0:lo, :] = jnp.zeros((lo, buf_ref.shape[1]), F32)
    buf_ref[CONV_HALO:, :] = u_ref[...]
    nch = u_ref.shape[1]
    span = CONV_HALO + tm - SUBLANES

    def chunk(ci, carry):
        c0 = pl.multiple_of(ci * CONV_LANE_CHUNK, CONV_LANE_CHUNK)
        lanes = pl.ds(c0, CONV_LANE_CHUNK)
        for b in range(1, SUBLANES):
            sh_ref[b, 0:span, :] = buf_ref[b:b + span, lanes]
        acc = jnp.broadcast_to(cb_ref[:, lanes], (tm, CONV_LANE_CHUNK))
        for w in range(CONV_WIDTH):
            a, b = divmod(lo + w, SUBLANES)
            if b == 0:
                rows = buf_ref[SUBLANES * a:SUBLANES * a + tm, lanes]
            else:
                rows = sh_ref[b, SUBLANES * a:SUBLANES * a + tm, :]
            acc = acc + rows * cw_ref[w:w + 1, lanes]
        y_ref[:, lanes] = acc
        return carry

    lax.fori_loop(0, nch // CONV_LANE_CHUNK, chunk, 0)

    y = y_ref[...]
    mu = jnp.mean(y, axis=-1, keepdims=True)
    yc = y - mu
    var = jnp.mean(yc * yc, axis=-1, keepdims=True)
    yn = yc * lax.rsqrt(var + EPS) * g_ref[...] + b_ref[...]
    sw = yn * jax.nn.sigmoid(yn)
    o_ref[...] = jnp.dot(sw.astype(BF16), w_ref[...], preferred_element_type=F32).astype(o_ref.dtype)


def _conv_module(u, state, nseq, conv_w, conv_b, ln_g, ln_b, w_pw2_bf, layer):
    t, ch = u.shape
    seq_len = t // nseq
    tm = _row_tile(seq_len, 256)
    assert tm % CONV_HALO == 0 and ch % CONV_LANE_CHUNK == 0
    tiles_per_seq = seq_len // tm
    halo_blocks = tm // CONV_HALO
    vec = lambda x: x.reshape(1, ch)
    const = lambda i: (0, 0)
    return pl.pallas_call(
        functools.partial(_conv_kernel, tm=tm, tiles_per_seq=tiles_per_seq),
        grid=(t // tm,),
        in_specs=[
            pl.BlockSpec((tm, ch), lambda i: (i, 0)),
            pl.BlockSpec((CONV_HALO, ch), lambda i: (jnp.maximum(i * halo_blocks - 1, 0), 0)),
            pl.BlockSpec((None, CONV_STATE, ch), lambda i: (i // tiles_per_seq, 0, 0)),
            pl.BlockSpec((CONV_WIDTH, ch), const),
            pl.BlockSpec((1, ch), const), pl.BlockSpec((1, ch), const), pl.BlockSpec((1, ch), const),
            pl.BlockSpec((None, ch, ch), lambda i: (layer, 0, 0)),
        ],
        out_specs=pl.BlockSpec((tm, ch), lambda i: (i, 0)),
        out_shape=jax.ShapeDtypeStruct((t, ch), BF16),
        scratch_shapes=[pltpu.VMEM((CONV_HALO + tm, ch), F32),
                        pltpu.VMEM((SUBLANES, CONV_HALO + tm, CONV_LANE_CHUNK), F32),
                        pltpu.VMEM((tm, ch), F32)],
        compiler_params=_params("parallel"),
        name="conv_module",
    )(u, u, state, conv_w, vec(conv_b), vec(ln_g), vec(ln_b), w_pw2_bf)


def _out_proj_kernel(x_ref, o_ref, c_ref, wt_ref, wb_ref, y_ref):
    acc = jnp.dot(o_ref[...], wt_ref[...], preferred_element_type=F32)
    acc = acc + jnp.dot(c_ref[...], wb_ref[...], preferred_element_type=F32)
    y_ref[...] = x_ref[...] + acc


def _out_proj(x, o, c, w_out_bf, layer):
    t, d = x.shape
    a = o.shape[1]
    assert c.shape[1] == a and 2 * a == d
    tm = _row_tile(t, 1024)
    tn = min(d, 512)
    return pl.pallas_call(
        _out_proj_kernel,
        grid=(t // tm, d // tn),
        in_specs=[
            pl.BlockSpec((tm, tn), lambda i, j: (i, j)),
            pl.BlockSpec((tm, a), lambda i, j: (i, 0)),
            pl.BlockSpec((tm, a), lambda i, j: (i, 0)),
            pl.BlockSpec((None, a, tn), lambda i, j: (layer, 0, j)),
            pl.BlockSpec((None, a, tn), lambda i, j: (layer, 1, j)),
        ],
        out_specs=pl.BlockSpec((tm, tn), lambda i, j: (i, j)),
        out_shape=jax.ShapeDtypeStruct((t, d), F32),
        compiler_params=_params("parallel", "parallel"),
        name="out_proj",
    )(x, o, c, w_out_bf, w_out_bf)


def _router_kernel(x_ref, g_ref, wr_ref, br_ref, h_ref, ri_ref, rw_ref, cnt_ref, carry_ref, *, n_experts, n_groups,
                   per_group):
    step = pl.program_id(0)

    @pl.when(step == 0)
    def _():
        carry_ref[...] = jnp.zeros(carry_ref.shape, F32)

    x = x_ref[...]
    ms = jnp.mean(x * x, axis=-1, keepdims=True)
    h = x * lax.rsqrt(ms + EPS) * g_ref[...]
    h_ref[...] = h.astype(h_ref.dtype)
    logits = jnp.dot(h, wr_ref[...], preferred_element_type=F32, precision=lax.Precision.HIGHEST) + br_ref[...]

    tm = x.shape[0]
    lane = lax.broadcasted_iota(jnp.int32, (tm, ROUTE_LANES), 1)
    big = jnp.int32(ROUTE_LANES)
    is_group = (lane >= n_experts) & (lane < n_experts + n_groups)
    gl = jnp.where(is_group, logits, -jnp.inf)
    gmax = jnp.max(gl, axis=1, keepdims=True)
    gidx = jnp.min(jnp.where(gl == gmax, lane, big), axis=1, keepdims=True) - n_experts
    g_gate = 1.0 / jnp.sum(jnp.exp(gl - gmax), axis=1, keepdims=True)

    in_group = (lane < n_experts) & ((lane // per_group) == gidx)
    e_in = jnp.where(in_group, logits, -jnp.inf)
    m1 = jnp.max(e_in, axis=1, keepdims=True)
    i1 = jnp.min(jnp.where(e_in == m1, lane, big), axis=1, keepdims=True)
    e_in2 = jnp.where(lane == i1, -jnp.inf, e_in)
    m2 = jnp.max(e_in2, axis=1, keepdims=True)
    i2 = jnp.min(jnp.where(e_in2 == m2, lane, big), axis=1, keepdims=True)
    t = jnp.exp(m2 - m1)
    w1 = g_gate / (1.0 + t)
    w2 = g_gate * t / (1.0 + t)

    sel1 = lane == i1
    sel2 = lane == i2
    onehot = jnp.where(sel1 | sel2, 1.0, 0.0).astype(BF16)
    r = lax.broadcasted_iota(jnp.int32, (tm, tm), 0)
    c = lax.broadcasted_iota(jnp.int32, (tm, tm), 1)
    lower = jnp.where(c < r, 1.0, 0.0).astype(BF16)
    before = jnp.dot(lower, onehot, preferred_element_type=F32) + carry_ref[...]
    rank1 = jnp.sum(jnp.where(sel1, before, 0.0), axis=1, keepdims=True).astype(jnp.int32)
    rank2 = jnp.sum(jnp.where(sel2, before, 0.0), axis=1, keepdims=True).astype(jnp.int32)
    carry_ref[...] = carry_ref[...] + jnp.sum(onehot.astype(F32), axis=0, keepdims=True)
    cnt_ref[...] = carry_ref[...].astype(jnp.int32)

    ri = jnp.where(lane == 0, i1, jnp.where(lane == 1, i2, jnp.where(lane == 2, rank1,
                                                                     jnp.where(lane == 3, rank2, 0))))
    ri_ref[...] = ri
    rw_ref[...] = jnp.where(lane == 0, w1, jnp.where(lane == 1, w2, 0.0))


def _router(x, norm_g, w_route, b_route, n_experts, n_groups):
    t, d = x.shape
    tm = _row_tile(t, 256)
    const = lambda i: (0, 0)
    return pl.pallas_call(
        functools.partial(_router_kernel, n_experts=n_experts, n_groups=n_groups,
                          per_group=n_experts // n_groups),
        grid=(t // tm,),
        in_specs=[
            pl.BlockSpec((tm, d), lambda i: (i, 0)),
            pl.BlockSpec((1, d), const),
            pl.BlockSpec((d, ROUTE_LANES), const),
            pl.BlockSpec((1, ROUTE_LANES), const),
        ],
        out_specs=[
            pl.BlockSpec((tm, d), lambda i: (i, 0)),
            pl.BlockSpec((tm, ROUTE_LANES), lambda i: (i, 0)),
            pl.BlockSpec((tm, ROUTE_LANES), lambda i: (i, 0)),
            pl.BlockSpec((1, ROUTE_LANES), const),
        ],
        out_shape=[
            jax.ShapeDtypeStruct((t, d), F32),
            jax.ShapeDtypeStruct((t, ROUTE_LANES), jnp.int32),
            jax.ShapeDtypeStruct((t, ROUTE_LANES), F32),
            jax.ShapeDtypeStruct((1, ROUTE_LANES), jnp.int32),
        ],
        scratch_shapes=[pltpu.VMEM((1, ROUTE_LANES), F32)],
        compiler_params=_params("arbitrary"),
        name="router",
    )(x, norm_g.reshape(1, d), w_route, b_route)


ROW_DMA_UNROLL = 8
MOE_TOKEN_BLOCK = 256


def _row_copy(src_ref, src_row, dst_ref, dst_row, sem):
    return pltpu.make_async_copy(src_ref.at[pl.ds(src_row, 1)], dst_ref.at[pl.ds(dst_row, 1)], sem)


def _scatter_rows(pos_ref, h_ref, xs_ref, sem, tb):
    n = TOP_K_IN_GROUP * tb

    def copy(r):
        return _row_copy(h_ref, r // TOP_K_IN_GROUP, xs_ref, pos_ref[0, 0, r], sem)

    def start(r, carry):
        copy(r).start()
        return carry

    def wait(r, carry):
        copy(r).wait()
        return carry

    lax.fori_loop(0, n, start, 0, unroll=ROW_DMA_UNROLL)
    lax.fori_loop(0, n, wait, 0, unroll=ROW_DMA_UNROLL)


def _dispatch_kernel(pos_ref, ends_ref, padded_ref, na_ref, *refs, tb, tm, blocks, n_tiles, min_active):
    n_streams = len(blocks)
    h_refs = refs[:n_streams]
    xs_ref, zero_ref, zsem, sem = refs[n_streams:]
    i = pl.program_id(0)

    @pl.when(i == 0)
    def _():
        zero_ref[...] = jnp.zeros(zero_ref.shape, zero_ref.dtype)

        def fill(first):
            return pltpu.make_async_copy(zero_ref, xs_ref.at[pl.ds(pl.multiple_of(first, tm), tm)], zsem)

        for wait in (False, True):
            for e in range(ends_ref.shape[0]):
                @pl.when(padded_ref[e] > 0)
                def _():
                    cp = fill(ends_ref[e] - tm)
                    cp.wait() if wait else cp.start()

            for tile in range(min_active, n_tiles):
                @pl.when(tile >= na_ref[0])
                def _():
                    cp = fill(tile * tm)
                    cp.wait() if wait else cp.start()

    first_block = 0
    for h_ref, nb in zip(h_refs, blocks):
        @pl.when((i >= first_block) & (i < first_block + nb))
        def _():
            _scatter_rows(pos_ref, h_ref, xs_ref, sem, tb)

        first_block += nb


def _dispatch(hs, positions, ends, padded, n_active, tm, n_tiles, min_active):
    d = hs[0].shape[1]
    tb = MOE_TOKEN_BLOCK
    blocks = [h.shape[0] // tb for h in hs]
    assert all(h.shape[0] % tb == 0 for h in hs)
    pos3 = jnp.concatenate([p.reshape(nb, 1, TOP_K_IN_GROUP * tb) for p, nb in zip(positions, blocks)], axis=0)
    smem = pl.BlockSpec(memory_space=pltpu.SMEM)
    h_specs, first_block = [], 0
    for nb in blocks:
        h_specs.append(pl.BlockSpec((tb, d), lambda i, f=first_block, n=nb: (jnp.clip(i - f, 0, n - 1), 0)))
        first_block += nb
    return pl.pallas_call(
        functools.partial(_dispatch_kernel, tb=tb, tm=tm, blocks=tuple(blocks), n_tiles=n_tiles,
                          min_active=min_active),
        grid=(sum(blocks),),
        in_specs=[pl.BlockSpec((1, 1, TOP_K_IN_GROUP * tb), lambda i: (i, 0, 0), memory_space=pltpu.SMEM),
                  smem, smem, smem] + h_specs,
        out_specs=pl.BlockSpec(memory_space=pl.ANY),
        out_shape=jax.ShapeDtypeStruct((n_tiles * tm, d), hs[0].dtype),
        scratch_shapes=[pltpu.VMEM((tm, d), hs[0].dtype), pltpu.SemaphoreType.DMA(()), pltpu.SemaphoreType.DMA(())],
        compiler_params=_params("arbitrary"),
        name="moe_dispatch",
    )(pos3, ends, padded, n_active, *hs)


def _experts_kernel(te_ref, tb_ref, na_ref, x_ref, w1_ref, w3_ref, w2_ref, y_ref):
    del te_ref, tb_ref
    active = pl.program_id(0) < na_ref[0]

    @pl.when(active)
    def _():
        x = x_ref[...].astype(BF16)
        a = jnp.dot(x, w1_ref[...], preferred_element_type=F32)
        b = jnp.dot(x, w3_ref[...], preferred_element_type=F32)
        he = (a * jax.nn.sigmoid(a)) * b
        y_ref[...] = jnp.dot(he.astype(BF16), w2_ref[...], preferred_element_type=F32)

    @pl.when(jnp.logical_not(active))
    def _():
        y_ref[...] = jnp.zeros(y_ref.shape, F32)


def _experts(xs, tile_expert, tile_block, n_active, w1_bf, w3_bf, w2_bf, layer, tm):
    n_slots, d = xs.shape
    ff = w1_bf.shape[-1]
    n_tiles = n_slots // tm
    row = lambda i, te, tb, na: (tb[i], 0)
    grid_spec = pltpu.PrefetchScalarGridSpec(
        num_scalar_prefetch=3,
        grid=(n_tiles,),
        in_specs=[
            pl.BlockSpec((tm, d), row),
            pl.BlockSpec((None, None, d, ff), lambda i, te, tb, na: (layer, te[i], 0, 0)),
            pl.BlockSpec((None, None, d, ff), lambda i, te, tb, na: (layer, te[i], 0, 0)),
            pl.BlockSpec((None, None, ff, d), lambda i, te, tb, na: (layer, te[i], 0, 0)),
        ],
        out_specs=pl.BlockSpec((tm, d), lambda i, te, tb, na: (i, 0)),
    )
    return pl.pallas_call(
        _experts_kernel,
        grid_spec=grid_spec,
        out_shape=jax.ShapeDtypeStruct((n_slots, d), F32),
        compiler_params=_params("arbitrary"),
        name="moe_experts",
    )(tile_expert, tile_block, n_active, xs, w1_bf, w3_bf, w2_bf)


def _combine_kernel(pos_ref, pos_next_ref, x_ref, rw_ref, ys_ref, o_ref, buf_ref, sems, *, tb):
    i = pl.program_id(0)
    n = TOP_K_IN_GROUP * tb
    slot = i % 2

    def copies(p_ref, s, start):
        def body(r, carry):
            cp = _row_copy(ys_ref, p_ref[0, 0, r], buf_ref.at[s, r % TOP_K_IN_GROUP], r // TOP_K_IN_GROUP, sems.at[s])
            if start:
                cp.start()
            else:
                cp.wait()
            return carry

        lax.fori_loop(0, n, body, 0, unroll=ROW_DMA_UNROLL)

    @pl.when(i == 0)
    def _():
        copies(pos_ref, slot, True)

    @pl.when(i + 1 < pl.num_programs(0))
    def _():
        copies(pos_next_ref, 1 - slot, True)

    copies(pos_ref, slot, False)
    rw = rw_ref[...]
    o_ref[...] = x_ref[...] + (rw[:, 0:1] * buf_ref[slot, 0] + rw[:, 1:2] * buf_ref[slot, 1])


def _combine(x, rw, pos, ys):
    t, d = x.shape
    tb = _row_tile(t, MOE_TOKEN_BLOCK)
    nb = t // tb
    pos3 = pos.reshape(nb, 1, TOP_K_IN_GROUP * tb)
    pos_block = (1, 1, TOP_K_IN_GROUP * tb)
    return pl.pallas_call(
        functools.partial(_combine_kernel, tb=tb),
        grid=(nb,),
        in_specs=[
            pl.BlockSpec(pos_block, lambda i: (i, 0, 0), memory_space=pltpu.SMEM),
            pl.BlockSpec(pos_block, lambda i: (jnp.minimum(i + 1, nb - 1), 0, 0), memory_space=pltpu.SMEM),
            pl.BlockSpec((tb, d), lambda i: (i, 0)),
            pl.BlockSpec((tb, ROUTE_LANES), lambda i: (i, 0)),
            pl.BlockSpec(memory_space=pl.ANY),
        ],
        out_specs=pl.BlockSpec((tb, d), lambda i: (i, 0)),
        out_shape=jax.ShapeDtypeStruct((t, d), F32),
        scratch_shapes=[pltpu.VMEM((2, TOP_K_IN_GROUP, tb, d), F32), pltpu.SemaphoreType.DMA((2,))],
        compiler_params=_params("arbitrary"),
        name="moe_combine",
    )(pos3, pos3, x, rw, ys)


MOE_ROW_TILE = 256


def _moe_block(streams, norm_g, w_route, b_route, w1_bf, w3_bf, w2_bf, layer, n_experts, n_groups):
    routed = [_router(x, norm_g, w_route, b_route, n_experts, n_groups) for x in streams]

    tm = MOE_ROW_TILE
    stream_counts = [counts[0, :n_experts] for _, _, _, counts in routed]
    counts = sum(stream_counts)
    padded = ((counts + tm - 1) // tm) * tm
    ends = jnp.cumsum(padded).astype(jnp.int32)
    offsets = ends - padded
    n_tokens = sum(x.shape[0] for x in streams)
    n_tiles = (TOP_K_IN_GROUP * n_tokens) // tm + n_experts
    n_active = (ends[-1] // tm).astype(jnp.int32)
    tile_ids = jnp.arange(n_tiles, dtype=jnp.int32)
    tile_block = jnp.minimum(tile_ids, n_active - 1)
    tile_expert = jnp.sum((ends[None, :] <= (tile_block * tm)[:, None]).astype(jnp.int32), axis=1)
    tile_expert = jnp.minimum(tile_expert, n_experts - 1)

    positions = []
    base = offsets
    for (_, ri, _, _), cnt in zip(routed, stream_counts):
        chosen = ri[:, 0:TOP_K_IN_GROUP]
        is_expert = chosen[:, :, None] == jnp.arange(n_experts, dtype=jnp.int32)
        pos = jnp.sum(jnp.where(is_expert, base, 0), axis=-1) + ri[:, TOP_K_IN_GROUP:2 * TOP_K_IN_GROUP]
        positions.append(pos.astype(jnp.int32))
        base = base + cnt
    n_active = n_active.reshape(1)
    xs = _dispatch([h for h, _, _, _ in routed], positions, ends, padded.astype(jnp.int32), n_active, tm, n_tiles,
                   (TOP_K_IN_GROUP * n_tokens) // tm)
    ys = _experts(xs, tile_expert, tile_block, n_active, w1_bf, w3_bf, w2_bf, layer, tm)
    return [_combine(x, rw, pos, ys) for x, (_, _, rw, _), pos in zip(streams, routed, positions)]


def _mixer_layer(x, layer, tables, lam, lam_init, attend, conv_state, nseq, p):
    a = p["attn_width"]
    ch = p["conv_ch"]
    h = _rmsnorm(x, p["norm1_g"][layer])
    q, = _qk_proj(h, p["w_in"], layer, 0, a, p["q_norm_g"][layer], tables, Q_SCALE, (BF16,))
    k32, k16 = _qk_proj(h, p["w_in"], layer, a, a, p["k_norm_g"][layer], tables, 1.0, (F32, BF16))
    v32, v16 = _v_proj(h, p["w_in"], layer, 2 * a, a)
    u = _glu_proj(h, p["w_in"], layer, 3 * a, 3 * a + ch, ch)
    o = attend(q, k16, v16, lam, p["subln_g"][layer], 1.0 - lam_init)
    c = _conv_module(u, conv_state, nseq, p["conv_w"][layer], p["conv_b"][layer], p["conv_ln_g"][layer],
                     p["conv_ln_b"][layer], p["w_pw2"], layer)
    x = _out_proj(x, o, c, p["w_out"], layer)
    seq_len = u.shape[0] // nseq
    assert seq_len >= CONV_STATE
    new_conv = u.reshape(nseq, seq_len, ch)[:, seq_len - CONV_STATE:]
    return x, k32, v32, new_conv


def kernel(x_prompt, x_sample, cache_k, cache_v, state_conv, norm1_g, w_in, q_norm_g, k_norm_g, lambda_q1, lambda_k1, lambda_q2, lambda_k2, subln_g, conv_w, conv_b, conv_ln_g, conv_ln_b, w_pw2, w_out, norm2_g, w_group_router, b_group_router, w_expert_router, b_expert_router, w1, w3, w2):
    depth, d, _ = w_in.shape
    nbp, seq, _ = x_prompt.shape
    nbs, dec_seq, _ = x_sample.shape
    past, nh, qk = cache_k.shape[2], cache_k.shape[3], cache_k.shape[4]
    a = nh * qk
    ch = state_conv.shape[-1]
    n_groups = w_group_router.shape[-1]
    n_experts = w_expert_router.shape[-1]
    assert nbp == 1 and qk == 2 * HEAD_DIM and n_experts + n_groups <= ROUTE_LANES

    pad = jnp.zeros((depth, d, ROUTE_LANES - n_experts - n_groups), F32)
    p = {
        "attn_width": a, "conv_ch": ch, "n_experts": n_experts, "n_groups": n_groups,
        "norm1_g": norm1_g, "q_norm_g": q_norm_g, "k_norm_g": k_norm_g, "subln_g": subln_g,
        "conv_w": conv_w, "conv_b": conv_b, "conv_ln_g": conv_ln_g, "conv_ln_b": conv_ln_b, "norm2_g": norm2_g,
        "w_in": w_in.astype(BF16), "w_pw2": w_pw2.astype(BF16), "w_out": w_out.astype(BF16),
        "w1": w1.astype(BF16), "w3": w3.astype(BF16), "w2": w2.astype(BF16),
        "w_route": jnp.concatenate([w_expert_router, w_group_router, pad], axis=-1),
        "b_route": jnp.concatenate([b_expert_router, b_group_router, pad[:, 0]], axis=-1)[:, None, :],
    }
    tables_p = _rope_tables(jnp.arange(seq))
    tables_s = _rope_tables(jnp.tile(past + jnp.arange(dec_seq), nbs))
    conv_zero = jnp.zeros((nbp, CONV_STATE, ch), F32)
    cache_k2 = cache_k.reshape(depth, nbs, past, a)
    cache_v2 = cache_v.reshape(depth, nbs, past, a)

    yp = x_prompt.reshape(nbp * seq, d)
    ys = x_sample.reshape(nbs * dec_seq, d)
    outs = [[] for _ in range(6)]
    for l in range(depth):
        lam_init = 0.8 - 0.6 * math.exp(-0.3 * l)
        lam = (jnp.exp(jnp.sum(lambda_q1[l] * lambda_k1[l])) - jnp.exp(jnp.sum(lambda_q2[l] * lambda_k2[l]))
               + lam_init).reshape(1).astype(F32)
        attend_s = functools.partial(_attn_sample_call, cache_k2, cache_v2, l)
        yp, kp, vp, cp = _mixer_layer(yp, l, tables_p, lam, lam_init, _attn_prompt, conv_zero, nbp, p)
        ys, kn, vn, cn = _mixer_layer(ys, l, tables_s, lam, lam_init, attend_s, state_conv[l], nbs, p)
        yp, ys = _moe_block([yp, ys], norm2_g[l], p["w_route"][l], p["b_route"][l], p["w1"], p["w3"], p["w2"], l,
                            n_experts, n_groups)
        for lst, val in zip(outs, (kp, vp, cp, kn, vn, cn)):
            lst.append(val)
    kp, vp, cp, kn, vn, cn = (jnp.stack(lst) for lst in outs)
    return (yp.reshape(nbp, seq, d), ys.reshape(nbs, dec_seq, d),
            kp.reshape(depth, nbp, seq, nh, qk), vp.reshape(depth, nbp, seq, nh, qk), cp,
            kn.reshape(depth, nbs, dec_seq, nh, qk), vn.reshape(depth, nbs, dec_seq, nh, qk), cn)


def _attn_sample_call(cache_k, cache_v, layer, q, k, v, lam, subln_g, out_scale):
    return _attn_sample(q, k, v, cache_k, cache_v, layer, lam, subln_g, out_scale)
```

```python
import functools
import math

import jax
import jax.numpy as jnp
from jax import lax
from jax.experimental import pallas as pl
from jax.experimental.pallas import tpu as pltpu

F32 = jnp.float32
BF16 = jnp.bfloat16

HEAD_DIM = 128
ROT_DIM = HEAD_DIM // 4
ROPE_THETA = 500000.0
CHUNK = 64
CONV_WIDTH = 31
CONV_STATE = CONV_WIDTH - 1
TOP_K_IN_GROUP = 2
EPS = 1e-6
NEG_INF = -1e30

LANES = 128
SUBLANES = 8
VMEM_LIMIT_BYTES = 56 * 1024 * 1024

CONV_HALO = 32
ROUTE_LANES = LANES


def _params(*semantics):
    return pltpu.CompilerParams(dimension_semantics=semantics, vmem_limit_bytes=VMEM_LIMIT_BYTES)


def _row_tile(t, cap):
    tile = min(t, cap)
    assert t % tile == 0, (t, tile)
    return tile


def _rmsnorm_kernel(x_ref, g_ref, o_ref):
    x = x_ref[...]
    ms = jnp.mean(x * x, axis=-1, keepdims=True)
    o_ref[...] = (x * lax.rsqrt(ms + EPS) * g_ref[...]).astype(o_ref.dtype)


def _rmsnorm(x, g):
    t, d = x.shape
    tm = _row_tile(t, 256)
    return pl.pallas_call(
        _rmsnorm_kernel,
        grid=(t // tm,),
        in_specs=[pl.BlockSpec((tm, d), lambda i: (i, 0)), pl.BlockSpec((1, d), lambda i: (0, 0))],
        out_specs=pl.BlockSpec((tm, d), lambda i: (i, 0)),
        out_shape=jax.ShapeDtypeStruct((t, d), BF16),
        compiler_params=_params("parallel"),
        name="rmsnorm",
    )(x, g.reshape(1, d))


def _rope_tables(pos):
    half = ROT_DIM // 2
    inv_freq = 1.0 / (ROPE_THETA ** (jnp.arange(half, dtype=F32) * 2.0 / ROT_DIM))
    ang = pos.astype(F32)[:, None] * inv_freq[None, :]
    cos, sin = jnp.cos(ang), jnp.sin(ang)
    t = pos.shape[0]
    c = jnp.concatenate([cos, cos, jnp.ones((t, HEAD_DIM - ROT_DIM), F32)], axis=-1)
    s = jnp.concatenate([-sin, sin, jnp.zeros((t, HEAD_DIM - ROT_DIM), F32)], axis=-1)
    return c, s


QK_PROJ_COLS = 256


def _qk_proj_kernel(h_ref, w_ref, g_ref, c_ref, s_ref, *out_refs, scale):
    h = h_ref[...]
    g, c, sn = g_ref[...], c_ref[...], s_ref[...]
    first_half = lax.broadcasted_iota(jnp.int32, c.shape, 1) < ROT_DIM // 2
    for q in range(w_ref.shape[1] // QK_PROJ_COLS):
        acc = jnp.dot(h, w_ref[:, q * QK_PROJ_COLS:(q + 1) * QK_PROJ_COLS], preferred_element_type=F32)
        for j in range(QK_PROJ_COLS // HEAD_DIM):
            x = acc[:, j * HEAD_DIM:(j + 1) * HEAD_DIM]
            ms = jnp.mean(x * x, axis=-1, keepdims=True)
            y = x * lax.rsqrt(ms + EPS) * g
            partner = jnp.where(first_half, pltpu.roll(y, HEAD_DIM - ROT_DIM // 2, 1),
                                pltpu.roll(y, ROT_DIM // 2, 1))
            r = y * c + partner * sn
            if scale != 1.0:
                r = r * scale
            cols = slice(q * QK_PROJ_COLS + j * HEAD_DIM, q * QK_PROJ_COLS + (j + 1) * HEAD_DIM)
            for o_ref in out_refs:
                o_ref[:, cols] = r.astype(o_ref.dtype)


def _qk_proj(h, w_bf, layer, col0, width, gain, tables, scale, out_dtypes):
    t, d = h.shape
    tm = _row_tile(t, 512)
    tn = min(width, 1024)
    assert width % tn == 0 and col0 % tn == 0 and tn % QK_PROJ_COLS == 0
    jb = col0 // tn
    c, s = tables
    tab_spec = pl.BlockSpec((tm, HEAD_DIM), lambda i, j: (i, 0))
    return pl.pallas_call(
        functools.partial(_qk_proj_kernel, scale=scale),
        grid=(t // tm, width // tn),
        in_specs=[
            pl.BlockSpec((tm, d), lambda i, j: (i, 0)),
            pl.BlockSpec((None, d, tn), lambda i, j: (layer, 0, jb + j)),
            pl.BlockSpec((1, HEAD_DIM), lambda i, j: (0, 0)),
            tab_spec, tab_spec,
        ],
        out_specs=[pl.BlockSpec((tm, tn), lambda i, j: (i, j)) for _ in out_dtypes],
        out_shape=[jax.ShapeDtypeStruct((t, width), dt) for dt in out_dtypes],
        compiler_params=_params("parallel", "parallel"),
        name="qk_proj",
    )(h, w_bf, gain.reshape(1, HEAD_DIM), c, s)


def _v_proj_kernel(h_ref, w_ref, o32_ref, o16_ref):
    acc = jnp.dot(h_ref[...], w_ref[...], preferred_element_type=F32)
    o32_ref[...] = acc
    o16_ref[...] = acc.astype(o16_ref.dtype)


def _v_proj(h, w_bf, layer, col0, width):
    t, d = h.shape
    tm = _row_tile(t, 1024)
    tn = min(width, 512)
    assert width % tn == 0 and col0 % tn == 0
    jb = col0 // tn
    return pl.pallas_call(
        _v_proj_kernel,
        grid=(t // tm, width // tn),
        in_specs=[
            pl.BlockSpec((tm, d), lambda i, j: (i, 0)),
            pl.BlockSpec((None, d, tn), lambda i, j: (layer, 0, jb + j)),
        ],
        out_specs=[pl.BlockSpec((tm, tn), lambda i, j: (i, j))] * 2,
        out_shape=[jax.ShapeDtypeStruct((t, width), F32), jax.ShapeDtypeStruct((t, width), BF16)],
        compiler_params=_params("parallel", "parallel"),
        name="v_proj",
    )(h, w_bf)


def _glu_proj_kernel(h_ref, wa_ref, wb_ref, u_ref):
    h = h_ref[...]
    a = jnp.dot(h, wa_ref[...], preferred_element_type=F32)
    b = jnp.dot(h, wb_ref[...], preferred_element_type=F32)
    u_ref[...] = a * jax.nn.sigmoid(b)


def _glu_proj(h, w_bf, layer, col_a, col_b, width):
    t, d = h.shape
    tm = _row_tile(t, 1024)
    tn = min(width, 512)
    assert width % tn == 0 and col_a % tn == 0 and col_b % tn == 0
    ja, jb = col_a // tn, col_b // tn
    return pl.pallas_call(
        _glu_proj_kernel,
        grid=(t // tm, width // tn),
        in_specs=[
            pl.BlockSpec((tm, d), lambda i, j: (i, 0)),
            pl.BlockSpec((None, d, tn), lambda i, j: (layer, 0, ja + j)),
            pl.BlockSpec((None, d, tn), lambda i, j: (layer, 0, jb + j)),
        ],
        out_specs=pl.BlockSpec((tm, tn), lambda i, j: (i, j)),
        out_shape=jax.ShapeDtypeStruct((t, width), F32),
        compiler_params=_params("parallel", "parallel"),
        name="glu_proj",
    )(h, w_bf, w_bf)


Q_SCALE = HEAD_DIM ** -0.5 * math.log2(math.e)
ATTN_BLOCK = 1024
ATTN_ROW_CHUNK = 512


def _stack_components(qs_ref, q, tq):
    z = jnp.zeros((tq, HEAD_DIM), q.dtype)
    qs_ref[0:tq, 0:HEAD_DIM] = q[:, 0:HEAD_DIM]
    qs_ref[0:tq, HEAD_DIM:] = z
    qs_ref[tq:, 0:HEAD_DIM] = z
    qs_ref[tq:, HEAD_DIM:] = q[:, HEAD_DIM:]


def _lane_repeat(x, n):
    return x if n == 1 else jnp.concatenate([x] * n, axis=1)


def _scores(qs, k):
    return lax.dot_general(qs, k, (((1,), (1,)), ((), ())), preferred_element_type=F32)


def _diff_subnorm(o1, o2, lam, g, out_scale):
    o = o1 - lam * o2
    ms = jnp.mean(o * o, axis=-1, keepdims=True)
    return o * lax.rsqrt(ms + EPS) * g * out_scale


def _attn_prompt_kernel(lam_ref, q_ref, k_ref, v_ref, g_ref, o_ref, qs_ref, s_ref, p_ref, mc_ref, m_ref, l_ref, acc_ref, *, tq,
                        rc,
                        out_scale):
    qi = pl.program_id(1)
    _stack_components(qs_ref, q_ref[...], tq)
    m_ref[...] = jnp.full(m_ref.shape, NEG_INF, F32)
    l_ref[...] = jnp.zeros(l_ref.shape, F32)
    acc_ref[...] = jnp.zeros(acc_ref.shape, F32)
    n_chunks = 2 * tq // rc

    def key_block(ref, ki):
        return ref[pl.ds(pl.multiple_of(ki * tq, tq), tq), :]

    def consume(c, v, masked):
        rows = slice(c * rc, (c + 1) * rc)
        s = s_ref[rows, :]
        if masked:
            row = lax.broadcasted_iota(jnp.int32, s.shape, 0) + (c * rc) % tq
            col = lax.broadcasted_iota(jnp.int32, s.shape, 1)
            s = jnp.where((col // CHUNK) <= (row // CHUNK), s, NEG_INF)
            m_cur = jnp.max(s, axis=1, keepdims=True)
        else:
            m_cur = mc_ref[rows, :]
        m_prev = m_ref[rows, :]
        m_new = jnp.maximum(m_prev, m_cur)
        alpha = jnp.exp2(m_prev - m_new)
        p = jnp.exp2(s - _lane_repeat(m_new, tq // LANES))
        l_ref[rows, :] = alpha * l_ref[rows, :] + jnp.sum(p, axis=1, keepdims=True)
        p_ref[rows, :] = p.astype(p_ref.dtype)
        pv = jnp.dot(p_ref[rows, :], v, preferred_element_type=F32)
        acc_ref[rows, :] = _lane_repeat(alpha, acc_ref.shape[1] // LANES) * acc_ref[rows, :] + pv
        m_ref[rows, :] = m_new

    def produce(c, k):
        rows = slice(c * rc, (c + 1) * rc)
        s = _scores(qs_ref[rows, :], k)
        s_ref[rows, :] = s
        mc_ref[rows, :] = jnp.broadcast_to(jnp.max(s, axis=1, keepdims=True), (rc, LANES))

    k0 = key_block(k_ref, 0)
    for c in range(n_chunks):
        produce(c, k0)

    def body(ki, carry):
        v = key_block(v_ref, ki)
        k_next = key_block(k_ref, ki + 1)
        for c in range(n_chunks):
            consume(c, v, False)
            produce(c, k_next)
        return carry

    lax.fori_loop(0, qi, body, 0)
    v = key_block(v_ref, qi)
    for c in range(n_chunks):
        consume(c, v, True)

    o = acc_ref[...] / _lane_repeat(l_ref[...], acc_ref.shape[1] // LANES)
    o_ref[...] = _diff_subnorm(o[0:tq], o[tq:], lam_ref[0], g_ref[...], out_scale).astype(o_ref.dtype)


def _attn_prompt(q, k, v, lam, subln_g, out_scale):
    t, a = q.shape
    qk = 2 * HEAD_DIM
    nh = a // qk
    tq = _row_tile(t, ATTN_BLOCK)
    assert tq % CHUNK == 0
    return pl.pallas_call(
        functools.partial(_attn_prompt_kernel, tq=tq, rc=min(tq, ATTN_ROW_CHUNK), out_scale=out_scale),
        grid=(nh, t // tq),
        in_specs=[
            pl.BlockSpec(memory_space=pltpu.SMEM),
            pl.BlockSpec((tq, qk), lambda h, i: (i, h)),
            pl.BlockSpec((t, qk), lambda h, i: (0, h), pipeline_mode=pl.Buffered(1)),
            pl.BlockSpec((t, qk), lambda h, i: (0, h), pipeline_mode=pl.Buffered(1)),
            pl.BlockSpec((1, qk), lambda h, i: (0, 0)),
        ],
        out_specs=pl.BlockSpec((tq, qk), lambda h, i: (i, h)),
        out_shape=jax.ShapeDtypeStruct((t, a), BF16),
        scratch_shapes=[
            pltpu.VMEM((2 * tq, qk), BF16),
            pltpu.VMEM((2 * tq, tq), F32),
            pltpu.VMEM((2 * tq, tq), BF16),
            pltpu.VMEM((2 * tq, LANES), F32),
            pltpu.VMEM((2 * tq, LANES), F32),
            pltpu.VMEM((2 * tq, LANES), F32),
            pltpu.VMEM((2 * tq, qk), F32),
        ],
        compiler_params=_params("parallel", "arbitrary"),
        name="attn_prompt",
    )(lam, q, k, v, subln_g.reshape(1, qk))


def _attn_sample_kernel(lam_ref, q_ref, kn_ref, vn_ref, kc_ref, vc_ref, g_ref, o_ref, qs_ref, *, ts, out_scale):
    _stack_components(qs_ref, q_ref[...], ts)
    qs = qs_ref[...]
    kc = kc_ref[...].astype(BF16)
    vc = vc_ref[...].astype(BF16)
    s_c = _scores(qs, kc)
    s_n = _scores(qs, kn_ref[...])
    m = jnp.maximum(jnp.max(s_c, axis=1, keepdims=True), jnp.max(s_n, axis=1, keepdims=True))
    p_c = jnp.exp2(s_c - m)
    p_n = jnp.exp2(s_n - m)
    l = jnp.sum(p_c, axis=1, keepdims=True) + jnp.sum(p_n, axis=1, keepdims=True)
    acc = jnp.dot(p_c.astype(BF16), vc, preferred_element_type=F32)
    acc = acc + jnp.dot(p_n.astype(BF16), vn_ref[...], preferred_element_type=F32)
    o = acc / l
    o_ref[...] = _diff_subnorm(o[0:ts], o[ts:], lam_ref[0], g_ref[...], out_scale).astype(o_ref.dtype)


def _attn_sample(q, k, v, cache_k, cache_v, layer, lam, subln_g, out_scale):
    _, nb, past, a = cache_k.shape
    ts = q.shape[0] // nb
    qk = 2 * HEAD_DIM
    nh = a // qk
    new_spec = pl.BlockSpec((ts, qk), lambda b, h: (b, h))
    cache_spec = pl.BlockSpec((None, None, past, qk), lambda b, h: (layer, b, 0, h))
    return pl.pallas_call(
        functools.partial(_attn_sample_kernel, ts=ts, out_scale=out_scale),
        grid=(nb, nh),
        in_specs=[pl.BlockSpec(memory_space=pltpu.SMEM), new_spec, new_spec, new_spec, cache_spec, cache_spec,
                  pl.BlockSpec((1, qk), lambda b, h: (0, 0))],
        out_specs=new_spec,
        out_shape=jax.ShapeDtypeStruct(q.shape, BF16),
        scratch_shapes=[pltpu.VMEM((2 * ts, qk), BF16)],
        compiler_params=_params("parallel", "parallel"),
        name="attn_sample",
    )(lam, q, k, v, cache_k, cache_v, subln_g.reshape(1, qk))


CONV_LANE_CHUNK = 128


def _conv_kernel(u_ref, prev_ref, state_ref, cw_ref, cb_ref, g_ref, b_ref, w_ref, o_ref, buf_ref, sh_ref, y_ref, *,
                 tm, tiles_per_seq):
    first = (pl.program_id(0) % tiles_per_seq) == 0
    lo = CONV_HALO - CONV_STATE

    @pl.when(first)
    def _():
        buf_ref[lo:CONV_HALO, :] = state_ref[...]

    @pl.when(jnp.logical_not(first))
    def _():
        buf_ref[lo:CONV_HALO, :] = prev_ref[lo:CONV_HALO, :]

    buf_ref[0:lo, :] = jnp.zeros((lo, buf_ref.shape[1]), F32)
    buf_ref[CONV_HALO:, :] = u_ref[...]
    nch = u_ref.shape[1]
    span = CONV_HALO + tm - SUBLANES

    def chunk(ci, carry):
        c0 = pl.multiple_of(ci * CONV_LANE_CHUNK, CONV_LANE_CHUNK)
        lanes = pl.ds(c0, CONV_LANE_CHUNK)
        for b in range(1, SUBLANES):
            sh_ref[b, 0:span, :] = buf_ref[b:b + span, lanes]
        acc = jnp.broadcast_to(cb_ref[:, lanes], (tm, CONV_LANE_CHUNK))
        for w in range(CONV_WIDTH):
            a, b = divmod(lo + w, SUBLANES)
            if b == 0:
                rows = buf_ref[SUBLANES * a:SUBLANES * a + tm, lanes]
            else:
                rows = sh_ref[b, SUBLANES * a:SUBLANES * a + tm, :]
            acc = acc + rows * cw_ref[w:w + 1, lanes]
        y_ref[:, lanes] = acc
        return carry

    lax.fori_loop(0, nch // CONV_LANE_CHUNK, chunk, 0)

    y = y_ref[...]
    mu = jnp.mean(y, axis=-1, keepdims=True)
    yc = y - mu
    var = jnp.mean(yc * yc, axis=-1, keepdims=True)
    yn = yc * lax.rsqrt(var + EPS) * g_ref[...] + b_ref[...]
    sw = yn * jax.nn.sigmoid(yn)
    o_ref[...] = jnp.dot(sw.astype(BF16), w_ref[...], preferred_element_type=F32).astype(o_ref.dtype)


def _conv_module(u, state, nseq, conv_w, conv_b, ln_g, ln_b, w_pw2_bf, layer):
    t, ch = u.shape
    seq_len = t // nseq
    tm = _row_tile(seq_len, 256)
    assert tm % CONV_HALO == 0 and ch % CONV_LANE_CHUNK == 0
    tiles_per_seq = seq_len // tm
    halo_blocks = tm // CONV_HALO
    vec = lambda x: x.reshape(1, ch)
    const = lambda i: (0, 0)
    return pl.pallas_call(
        functools.partial(_conv_kernel, tm=tm, tiles_per_seq=tiles_per_seq),
        grid=(t // tm,),
        in_specs=[
            pl.BlockSpec((tm, ch), lambda i: (i, 0)),
            pl.BlockSpec((CONV_HALO, ch), lambda i: (jnp.maximum(i * halo_blocks - 1, 0), 0)),
            pl.BlockSpec((None, CONV_STATE, ch), lambda i: (i // tiles_per_seq, 0, 0)),
            pl.BlockSpec((CONV_WIDTH, ch), const),
            pl.BlockSpec((1, ch), const), pl.BlockSpec((1, ch), const), pl.BlockSpec((1, ch), const),
            pl.BlockSpec((None, ch, ch), lambda i: (layer, 0, 0)),
        ],
        out_specs=pl.BlockSpec((tm, ch), lambda i: (i, 0)),
        out_shape=jax.ShapeDtypeStruct((t, ch), BF16),
        scratch_shapes=[pltpu.VMEM((CONV_HALO + tm, ch), F32),
                        pltpu.VMEM((SUBLANES, CONV_HALO + tm, CONV_LANE_CHUNK), F32),
                        pltpu.VMEM((tm, ch), F32)],
        compiler_params=_params("parallel"),
        name="conv_module",
    )(u, u, state, conv_w, vec(conv_b), vec(ln_g), vec(ln_b), w_pw2_bf)


def _out_proj_kernel(x_ref, o_ref, c_ref, wt_ref, wb_ref, y_ref):
    acc = jnp.dot(o_ref[...], wt_ref[...], preferred_element_type=F32)
    acc = acc + jnp.dot(c_ref[...], wb_ref[...], preferred_element_type=F32)
    y_ref[...] = x_ref[...] + acc


def _out_proj(x, o, c, w_out_bf, layer):
    t, d = x.shape
    a = o.shape[1]
    assert c.shape[1] == a and 2 * a == d
    tm = _row_tile(t, 1024)
    tn = min(d, 512)
    return pl.pallas_call(
        _out_proj_kernel,
        grid=(t // tm, d // tn),
        in_specs=[
            pl.BlockSpec((tm, tn), lambda i, j: (i, j)),
            pl.BlockSpec((tm, a), lambda i, j: (i, 0)),
            pl.BlockSpec((tm, a), lambda i, j: (i, 0)),
            pl.BlockSpec((None, a, tn), lambda i, j: (layer, 0, j)),
            pl.BlockSpec((None, a, tn), lambda i, j: (layer, 1, j)),
        ],
        out_specs=pl.BlockSpec((tm, tn), lambda i, j: (i, j)),
        out_shape=jax.ShapeDtypeStruct((t, d), F32),
        compiler_params=_params("parallel", "parallel"),
        name="out_proj",
    )(x, o, c, w_out_bf, w_out_bf)


def _pack_bf16_pairs(x):
    half = x.shape[1] // 2

    def bf16_bits(v):
        bits = lax.bitcast_convert_type(v, jnp.uint32)
        return (bits + jnp.uint32(0x7FFF) + ((bits >> 16) & jnp.uint32(1))) >> 16

    return bf16_bits(x[:, :half]) | (bf16_bits(x[:, half:]) << 16)


def _unpack_bf16_pairs(packed):
    lo = lax.bitcast_convert_type(packed << 16, F32)
    hi = lax.bitcast_convert_type(packed & jnp.uint32(0xFFFF0000), F32)
    return jnp.concatenate([lo, hi], axis=1).astype(BF16)


def _router_kernel(x_ref, g_ref, wr_ref, br_ref, h_ref, ri_ref, rw_ref, cnt_ref, carry_ref, *, n_experts, n_groups,
                   per_group):
    step = pl.program_id(0)

    @pl.when(step == 0)
    def _():
        carry_ref[...] = jnp.zeros(carry_ref.shape, F32)

    x = x_ref[...]
    ms = jnp.mean(x * x, axis=-1, keepdims=True)
    h = x * lax.rsqrt(ms + EPS) * g_ref[...]
    h_ref[...] = _pack_bf16_pairs(h)
    logits = jnp.dot(h, wr_ref[...], preferred_element_type=F32, precision=lax.Precision.HIGHEST) + br_ref[...]

    tm = x.shape[0]
    lane = lax.broadcasted_iota(jnp.int32, (tm, ROUTE_LANES), 1)
    big = jnp.int32(ROUTE_LANES)
    is_group = (lane >= n_experts) & (lane < n_experts + n_groups)
    gl = jnp.where(is_group, logits, -jnp.inf)
    gmax = jnp.max(gl, axis=1, keepdims=True)
    gidx = jnp.min(jnp.where(gl == gmax, lane, big), axis=1, keepdims=True) - n_experts
    g_gate = 1.0 / jnp.sum(jnp.exp(gl - gmax), axis=1, keepdims=True)

    in_group = (lane < n_experts) & ((lane // per_group) == gidx)
    e_in = jnp.where(in_group, logits, -jnp.inf)
    m1 = jnp.max(e_in, axis=1, keepdims=True)
    i1 = jnp.min(jnp.where(e_in == m1, lane, big), axis=1, keepdims=True)
    e_in2 = jnp.where(lane == i1, -jnp.inf, e_in)
    m2 = jnp.max(e_in2, axis=1, keepdims=True)
    i2 = jnp.min(jnp.where(e_in2 == m2, lane, big), axis=1, keepdims=True)
    t = jnp.exp(m2 - m1)
    w1 = g_gate / (1.0 + t)
    w2 = g_gate * t / (1.0 + t)

    sel1 = lane == i1
    sel2 = lane == i2
    onehot = jnp.where(sel1 | sel2, 1.0, 0.0).astype(BF16)
    r = lax.broadcasted_iota(jnp.int32, (tm, tm), 0)
    c = lax.broadcasted_iota(jnp.int32, (tm, tm), 1)
    lower = jnp.where(c < r, 1.0, 0.0).astype(BF16)
    before = jnp.dot(lower, onehot, preferred_element_type=F32) + carry_ref[...]
    rank1 = jnp.sum(jnp.where(sel1, before, 0.0), axis=1, keepdims=True).astype(jnp.int32)
    rank2 = jnp.sum(jnp.where(sel2, before, 0.0), axis=1, keepdims=True).astype(jnp.int32)
    carry_ref[...] = carry_ref[...] + jnp.sum(onehot.astype(F32), axis=0, keepdims=True)
    cnt_ref[...] = carry_ref[...].astype(jnp.int32)

    ri = jnp.where(lane == 0, i1, jnp.where(lane == 1, i2, jnp.where(lane == 2, rank1,
                                                                     jnp.where(lane == 3, rank2, 0))))
    ri_ref[...] = ri
    rw_ref[...] = jnp.where(lane == 0, w1, jnp.where(lane == 1, w2, 0.0))


def _router(x, norm_g, w_route, b_route, n_experts, n_groups):
    t, d = x.shape
    tm = _row_tile(t, 256)
    const = lambda i: (0, 0)
    return pl.pallas_call(
        functools.partial(_router_kernel, n_experts=n_experts, n_groups=n_groups,
                          per_group=n_experts // n_groups),
        grid=(t // tm,),
        in_specs=[
            pl.BlockSpec((tm, d), lambda i: (i, 0)),
            pl.BlockSpec((1, d), const),
            pl.BlockSpec((d, ROUTE_LANES), const),
            pl.BlockSpec((1, ROUTE_LANES), const),
        ],
        out_specs=[
            pl.BlockSpec((tm, d // 2), lambda i: (i, 0)),
            pl.BlockSpec((tm, ROUTE_LANES), lambda i: (i, 0)),
            pl.BlockSpec((tm, ROUTE_LANES), lambda i: (i, 0)),
            pl.BlockSpec((1, ROUTE_LANES), const),
        ],
        out_shape=[
            jax.ShapeDtypeStruct((t, d // 2), jnp.uint32),
            jax.ShapeDtypeStruct((t, ROUTE_LANES), jnp.int32),
            jax.ShapeDtypeStruct((t, ROUTE_LANES), F32),
            jax.ShapeDtypeStruct((1, ROUTE_LANES), jnp.int32),
        ],
        scratch_shapes=[pltpu.VMEM((1, ROUTE_LANES), F32)],
        compiler_params=_params("arbitrary"),
        name="router",
    )(x, norm_g.reshape(1, d), w_route, b_route)


ROW_DMA_UNROLL = 8
MOE_TOKEN_BLOCK = 256


def _row_copy(src_ref, src_row, dst_ref, dst_row, sem):
    return pltpu.make_async_copy(src_ref.at[pl.ds(src_row, 1)], dst_ref.at[pl.ds(dst_row, 1)], sem)


def _scatter_rows(pos_ref, h_ref, xs_ref, sem, tb):
    def copy(t, k):
        return _row_copy(h_ref, t, xs_ref, pos_ref[0, 0, TOP_K_IN_GROUP * t + k], sem)

    def start(t, carry):
        for k in range(TOP_K_IN_GROUP):
            copy(t, k).start()
        return carry

    def wait(t, carry):
        for k in range(TOP_K_IN_GROUP):
            copy(t, k).wait()
        return carry

    lax.fori_loop(0, tb, start, 0, unroll=ROW_DMA_UNROLL)
    lax.fori_loop(0, tb, wait, 0, unroll=ROW_DMA_UNROLL)


def _dispatch_kernel(pos_ref, ends_ref, padded_ref, na_ref, *refs, tb, tm, blocks, n_tiles, min_active):
    n_streams = len(blocks)
    h_refs = refs[:n_streams]
    xs_ref, zero_ref, zsem, sem = refs[n_streams:]
    i = pl.program_id(0)

    @pl.when(i == 0)
    def _():
        zero_ref[...] = jnp.zeros(zero_ref.shape, zero_ref.dtype)

        def fill(first):
            return pltpu.make_async_copy(zero_ref, xs_ref.at[pl.ds(pl.multiple_of(first, tm), tm)], zsem)

        for wait in (False, True):
            for e in range(ends_ref.shape[0]):
                @pl.when(padded_ref[e] > 0)
                def _():
                    cp = fill(ends_ref[e] - tm)
                    cp.wait() if wait else cp.start()

            for tile in range(min_active, n_tiles):
                @pl.when(tile >= na_ref[0])
                def _():
                    cp = fill(tile * tm)
                    cp.wait() if wait else cp.start()

    first_block = 0
    for h_ref, nb in zip(h_refs, blocks):
        @pl.when((i >= first_block) & (i < first_block + nb))
        def _():
            _scatter_rows(pos_ref, h_ref, xs_ref, sem, tb)

        first_block += nb


def _dispatch(hs, positions, ends, padded, n_active, tm, n_tiles, min_active):
    d = hs[0].shape[1]
    tb = MOE_TOKEN_BLOCK
    blocks = [h.shape[0] // tb for h in hs]
    assert all(h.shape[0] % tb == 0 for h in hs)
    pos3 = jnp.concatenate([p.reshape(nb, 1, TOP_K_IN_GROUP * tb) for p, nb in zip(positions, blocks)], axis=0)
    smem = pl.BlockSpec(memory_space=pltpu.SMEM)
    h_specs, first_block = [], 0
    for nb in blocks:
        h_specs.append(pl.BlockSpec((tb, d), lambda i, f=first_block, n=nb: (jnp.clip(i - f, 0, n - 1), 0)))
        first_block += nb
    return pl.pallas_call(
        functools.partial(_dispatch_kernel, tb=tb, tm=tm, blocks=tuple(blocks), n_tiles=n_tiles,
                          min_active=min_active),
        grid=(sum(blocks),),
        in_specs=[pl.BlockSpec((1, 1, TOP_K_IN_GROUP * tb), lambda i: (i, 0, 0), memory_space=pltpu.SMEM),
                  smem, smem, smem] + h_specs,
        out_specs=pl.BlockSpec(memory_space=pl.ANY),
        out_shape=jax.ShapeDtypeStruct((n_tiles * tm, d), hs[0].dtype),
        scratch_shapes=[pltpu.VMEM((tm, d), hs[0].dtype), pltpu.SemaphoreType.DMA(()), pltpu.SemaphoreType.DMA(())],
        compiler_params=_params("arbitrary"),
        name="moe_dispatch",
    )(pos3, ends, padded, n_active, *hs)


def _experts_kernel(te_ref, tb_ref, na_ref, x_ref, w1_ref, w3_ref, w2_ref, y_ref):
    del te_ref, tb_ref
    active = pl.program_id(0) < na_ref[0]

    @pl.when(active)
    def _():
        x = _unpack_bf16_pairs(x_ref[...])
        a = jnp.dot(x, w1_ref[...], preferred_element_type=F32)
        b = jnp.dot(x, w3_ref[...], preferred_element_type=F32)
        he = (a * jax.nn.sigmoid(a)) * b
        y_ref[...] = jnp.dot(he.astype(BF16), w2_ref[...], preferred_element_type=F32)

    @pl.when(jnp.logical_not(active))
    def _():
        y_ref[...] = jnp.zeros(y_ref.shape, F32)


def _experts(xs, tile_expert, tile_block, n_active, w1_bf, w3_bf, w2_bf, layer, tm):
    n_slots, d_packed = xs.shape
    d = 2 * d_packed
    ff = w1_bf.shape[-1]
    n_tiles = n_slots // tm
    row = lambda i, te, tb, na: (tb[i], 0)
    grid_spec = pltpu.PrefetchScalarGridSpec(
        num_scalar_prefetch=3,
        grid=(n_tiles,),
        in_specs=[
            pl.BlockSpec((tm, d_packed), row),
            pl.BlockSpec((None, None, d, ff), lambda i, te, tb, na: (layer, te[i], 0, 0)),
            pl.BlockSpec((None, None, d, ff), lambda i, te, tb, na: (layer, te[i], 0, 0)),
            pl.BlockSpec((None, None, ff, d), lambda i, te, tb, na: (layer, te[i], 0, 0)),
        ],
        out_specs=pl.BlockSpec((tm, d), lambda i, te, tb, na: (i, 0)),
    )
    return pl.pallas_call(
        _experts_kernel,
        grid_spec=grid_spec,
        out_shape=jax.ShapeDtypeStruct((n_slots, d), F32),
        compiler_params=_params("arbitrary"),
        name="moe_experts",
    )(tile_expert, tile_block, n_active, xs, w1_bf, w3_bf, w2_bf)


def _combine_kernel(pos_ref, pos_next_ref, x_ref, rw_ref, ys_ref, o_ref, buf_ref, sems, *, tb):
    i = pl.program_id(0)
    slot = i & 1

    def copies(p_ref, s, start):
        def body(t, carry):
            for k in range(TOP_K_IN_GROUP):
                cp = _row_copy(ys_ref, p_ref[0, 0, TOP_K_IN_GROUP * t + k], buf_ref.at[s, k], t, sems.at[s])
                if start:
                    cp.start()
                else:
                    cp.wait()
            return carry

        lax.fori_loop(0, tb, body, 0, unroll=ROW_DMA_UNROLL)

    @pl.when(i == 0)
    def _():
        copies(pos_ref, slot, True)

    @pl.when(i + 1 < pl.num_programs(0))
    def _():
        copies(pos_next_ref, 1 - slot, True)

    copies(pos_ref, slot, False)
    rw = rw_ref[...]
    o_ref[...] = x_ref[...] + (rw[:, 0:1] * buf_ref[slot, 0] + rw[:, 1:2] * buf_ref[slot, 1])


def _combine(x, rw, pos, ys):
    t, d = x.shape
    tb = _row_tile(t, MOE_TOKEN_BLOCK)
    nb = t // tb
    pos3 = pos.reshape(nb, 1, TOP_K_IN_GROUP * tb)
    pos_block = (1, 1, TOP_K_IN_GROUP * tb)
    return pl.pallas_call(
        functools.partial(_combine_kernel, tb=tb),
        grid=(nb,),
        in_specs=[
            pl.BlockSpec(pos_block, lambda i: (i, 0, 0), memory_space=pltpu.SMEM),
            pl.BlockSpec(pos_block, lambda i: (jnp.minimum(i + 1, nb - 1), 0, 0), memory_space=pltpu.SMEM),
            pl.BlockSpec((tb, d), lambda i: (i, 0)),
            pl.BlockSpec((tb, ROUTE_LANES), lambda i: (i, 0)),
            pl.BlockSpec(memory_space=pl.ANY),
        ],
        out_specs=pl.BlockSpec((tb, d), lambda i: (i, 0)),
        out_shape=jax.ShapeDtypeStruct((t, d), F32),
        scratch_shapes=[pltpu.VMEM((2, TOP_K_IN_GROUP, tb, d), F32), pltpu.SemaphoreType.DMA((2,))],
        compiler_params=_params("arbitrary"),
        name="moe_combine",
    )(pos3, pos3, x, rw, ys)


MOE_ROW_TILE = 256


def _moe_block(streams, norm_g, w_route, b_route, w1_bf, w3_bf, w2_bf, layer, n_experts, n_groups):
    routed = [_router(x, norm_g, w_route, b_route, n_experts, n_groups) for x in streams]

    tm = MOE_ROW_TILE
    stream_counts = [counts[0, :n_experts] for _, _, _, counts in routed]
    counts = sum(stream_counts)
    padded = ((counts + tm - 1) // tm) * tm
    ends = jnp.cumsum(padded).astype(jnp.int32)
    offsets = ends - padded
    n_tokens = sum(x.shape[0] for x in streams)
    n_tiles = (TOP_K_IN_GROUP * n_tokens) // tm + n_experts
    n_active = (ends[-1] // tm).astype(jnp.int32)
    tile_ids = jnp.arange(n_tiles, dtype=jnp.int32)
    tile_block = jnp.minimum(tile_ids, n_active - 1)
    tile_expert = jnp.sum((ends[None, :] <= (tile_block * tm)[:, None]).astype(jnp.int32), axis=1)
    tile_expert = jnp.minimum(tile_expert, n_experts - 1)

    positions = []
    base = offsets
    for (_, ri, _, _), cnt in zip(routed, stream_counts):
        chosen = ri[:, 0:TOP_K_IN_GROUP]
        is_expert = chosen[:, :, None] == jnp.arange(n_experts, dtype=jnp.int32)
        pos = jnp.sum(jnp.where(is_expert, base, 0), axis=-1) + ri[:, TOP_K_IN_GROUP:2 * TOP_K_IN_GROUP]
        positions.append(pos.astype(jnp.int32))
        base = base + cnt
    n_active = n_active.reshape(1)
    xs = _dispatch([h for h, _, _, _ in routed], positions, ends, padded.astype(jnp.int32), n_active, tm, n_tiles,
                   (TOP_K_IN_GROUP * n_tokens) // tm)
    ys = _experts(xs, tile_expert, tile_block, n_active, w1_bf, w3_bf, w2_bf, layer, tm)
    return [_combine(x, rw, pos, ys) for x, (_, _, rw, _), pos in zip(streams, routed, positions)]


def _mixer_layer(x, layer, tables, lam, lam_init, attend, conv_state, nseq, p):
    a = p["attn_width"]
    ch = p["conv_ch"]
    h = _rmsnorm(x, p["norm1_g"][layer])
    q, = _qk_proj(h, p["w_in"], layer, 0, a, p["q_norm_g"][layer], tables, Q_SCALE, (BF16,))
    k32, k16 = _qk_proj(h, p["w_in"], layer, a, a, p["k_norm_g"][layer], tables, 1.0, (F32, BF16))
    v32, v16 = _v_proj(h, p["w_in"], layer, 2 * a, a)
    u = _glu_proj(h, p["w_in"], layer, 3 * a, 3 * a + ch, ch)
    o = attend(q, k16, v16, lam, p["subln_g"][layer], 1.0 - lam_init)
    c = _conv_module(u, conv_state, nseq, p["conv_w"][layer], p["conv_b"][layer], p["conv_ln_g"][layer],
                     p["conv_ln_b"][layer], p["w_pw2"], layer)
    x = _out_proj(x, o, c, p["w_out"], layer)
    seq_len = u.shape[0] // nseq
    assert seq_len >= CONV_STATE
    new_conv = u.reshape(nseq, seq_len, ch)[:, seq_len - CONV_STATE:]
    return x, k32, v32, new_conv


def kernel(x_prompt, x_sample, cache_k, cache_v, state_conv, norm1_g, w_in, q_norm_g, k_norm_g, lambda_q1, lambda_k1, lambda_q2, lambda_k2, subln_g, conv_w, conv_b, conv_ln_g, conv_ln_b, w_pw2, w_out, norm2_g, w_group_router, b_group_router, w_expert_router, b_expert_router, w1, w3, w2):
    depth, d, _ = w_in.shape
    nbp, seq, _ = x_prompt.shape
    nbs, dec_seq, _ = x_sample.shape
    past, nh, qk = cache_k.shape[2], cache_k.shape[3], cache_k.shape[4]
    a = nh * qk
    ch = state_conv.shape[-1]
    n_groups = w_group_router.shape[-1]
    n_experts = w_expert_router.shape[-1]
    assert nbp == 1 and qk == 2 * HEAD_DIM and n_experts + n_groups <= ROUTE_LANES

    pad = jnp.zeros((depth, d, ROUTE_LANES - n_experts - n_groups), F32)
    p = {
        "attn_width": a, "conv_ch": ch, "n_experts": n_experts, "n_groups": n_groups,
        "norm1_g": norm1_g, "q_norm_g": q_norm_g, "k_norm_g": k_norm_g, "subln_g": subln_g,
        "conv_w": conv_w, "conv_b": conv_b, "conv_ln_g": conv_ln_g, "conv_ln_b": conv_ln_b, "norm2_g": norm2_g,
        "w_in": w_in.astype(BF16), "w_pw2": w_pw2.astype(BF16), "w_out": w_out.astype(BF16),
        "w1": w1.astype(BF16), "w3": w3.astype(BF16), "w2": w2.astype(BF16),
        "w_route": jnp.concatenate([w_expert_router, w_group_router, pad], axis=-1),
        "b_route": jnp.concatenate([b_expert_router, b_group_router, pad[:, 0]], axis=-1)[:, None, :],
    }
    tables_p = _rope_tables(jnp.arange(seq))
    tables_s = _rope_tables(jnp.tile(past + jnp.arange(dec_seq), nbs))
    conv_zero = jnp.zeros((nbp, CONV_STATE, ch), F32)
    cache_k2 = cache_k.reshape(depth, nbs, past, a)
    cache_v2 = cache_v.reshape(depth, nbs, past, a)

    yp = x_prompt.reshape(nbp * seq, d)
    ys = x_sample.reshape(nbs * dec_seq, d)
    outs = [[] for _ in range(6)]
    for l in range(depth):
        lam_init = 0.8 - 0.6 * math.exp(-0.3 * l)
        lam = (jnp.exp(jnp.sum(lambda_q1[l] * lambda_k1[l])) - jnp.exp(jnp.sum(lambda_q2[l] * lambda_k2[l]))
               + lam_init).reshape(1).astype(F32)
        attend_s = functools.partial(_attn_sample_call, cache_k2, cache_v2, l)
        yp, kp, vp, cp = _mixer_layer(yp, l, tables_p, lam, lam_init, _attn_prompt, conv_zero, nbp, p)
        ys, kn, vn, cn = _mixer_layer(ys, l, tables_s, lam, lam_init, attend_s, state_conv[l], nbs, p)
        yp, ys = _moe_block([yp, ys], norm2_g[l], p["w_route"][l], p["b_route"][l], p["w1"], p["w3"], p["w2"], l,
                            n_experts, n_groups)
        for lst, val in zip(outs, (kp, vp, cp, kn, vn, cn)):
            lst.append(val)
    kp, vp, cp, kn, vn, cn = (jnp.stack(lst) for lst in outs)
    return (yp.reshape(nbp, seq, d), ys.reshape(nbs, dec_seq, d),
            kp.reshape(depth, nbp, seq, nh, qk), vp.reshape(depth, nbp, seq, nh, qk), cp,
            kn.reshape(depth, nbs, dec_seq, nh, qk), vn.reshape(depth, nbs, dec_seq, nh, qk), cn)


def _attn_sample_call(cache_k, cache_v, layer, q, k, v, lam, subln_g, out_scale):
    return _attn_sample(q, k, v, cache_k, cache_v, layer, lam, subln_g, out_scale)
```
